```python
import jax, jax.numpy as jnp
from jax import lax
import numpy as np

D_MODEL = 1024
BATCH = 8
SEQ = 2048
DEPTH = 4

N_MIXERS = 3
EPS = 1e-6
CONF_KERNEL = 31
GDN_HEADS = 8
GDN_HEAD_DIM = D_MODEL // GDN_HEADS
GDN_CONV = 4
GDN_CHUNK = 64
FOX_HEADS = 8
FOX_HEAD_DIM = D_MODEL // FOX_HEADS
FOX_BLOCK = 128
D_FF = ((8 * D_MODEL // 3 + 127) // 128) * 128
FFN_CONV = 3

kernel_name = "hybrid_conformer_gdn_fox_trunk"


def _rms_norm(x, g):
    xf = x.astype(jnp.float32)
    y = xf * lax.rsqrt(jnp.mean(xf * xf, axis=-1, keepdims=True) + EPS)
    return (y * g.astype(jnp.float32)).astype(x.dtype)


def _layer_norm(x, g, b):
    xf = x.astype(jnp.float32)
    xc = xf - jnp.mean(xf, axis=-1, keepdims=True)
    var = jnp.mean(xc * xc, axis=-1, keepdims=True)
    return (xc * lax.rsqrt(var + EPS) * g.astype(jnp.float32) + b.astype(jnp.float32)).astype(x.dtype)


def _l2norm(x):
    xf = x.astype(jnp.float32)
    return xf * lax.rsqrt(jnp.sum(xf * xf, axis=-1, keepdims=True) + EPS)


def _causal_dwconv(x, w):
    K, C = w.shape
    return lax.conv_general_dilated(
        x, w[:, None, :].astype(x.dtype), window_strides=(1,), padding=[(K - 1, 0)],
        dimension_numbers=("NWC", "WIO", "NWC"), feature_group_count=C)


def conformer_conv(h, w_in, b_in, w_dw, b_dw, ln_g, ln_b, w_out):
    u = h @ w_in + b_in
    val, gate = jnp.split(u, 2, axis=-1)
    u = val * jax.nn.sigmoid(gate)
    u = _causal_dwconv(u, w_dw) + b_dw
    u = jax.nn.silu(_layer_norm(u, ln_g, ln_b))
    return u @ w_out


def _chunk_gated_delta(q, k, v, g, beta):
    bsz, seq, H, Dh = q.shape
    C = GDN_CHUNK
    N = seq // C
    to_chunks = lambda t: t.reshape(bsz, N, C, H, -1).transpose(1, 0, 3, 2, 4)
    q, k, v = to_chunks(q), to_chunks(k), to_chunks(v)
    g = g.reshape(bsz, N, C, H).transpose(1, 0, 3, 2)
    beta = beta.reshape(bsz, N, C, H).transpose(1, 0, 3, 2)
    g = jnp.cumsum(g, axis=-1)
    kb = k * beta[..., None]
    vb = v * beta[..., None]
    idx = jnp.arange(C)
    lower = idx[:, None] >= idx[None, :]
    strict = idx[:, None] > idx[None, :]
    diff = g[..., :, None] - g[..., None, :]
    decay = jnp.where(lower, jnp.exp(jnp.where(lower, diff, 0.0)), 0.0)
    a_mat = jnp.where(strict, jnp.einsum("nbhid,nbhjd->nbhij", kb, k) * decay, 0.0)
    eye = jnp.eye(C, dtype=jnp.float32)
    t_mat = lax.linalg.triangular_solve(eye + a_mat, jnp.broadcast_to(eye, a_mat.shape),
                                        left_side=True, lower=True)
    u = jnp.einsum("nbhij,nbhjd->nbhid", t_mat, vb)
    w = jnp.einsum("nbhij,nbhjd->nbhid", t_mat, kb * jnp.exp(g)[..., None])
    qk = jnp.where(lower, jnp.einsum("nbhid,nbhjd->nbhij", q, k) * decay, 0.0)
    qg = q * jnp.exp(g)[..., None]
    kd = k * jnp.exp(g[..., -1:] - g)[..., None]
    g_last = jnp.exp(g[..., -1])

    def step(state, xs):
        u_n, w_n, qg_n, qk_n, kd_n, gl_n = xs
        v_new = u_n - jnp.einsum("bhck,bhkv->bhcv", w_n, state)
        o_n = jnp.einsum("bhck,bhkv->bhcv", qg_n, state) + jnp.einsum("bhij,bhjv->bhiv", qk_n, v_new)
        state = state * gl_n[..., None, None] + jnp.einsum("bhck,bhcv->bhkv", kd_n, v_new)
        return state, o_n

    s0 = jnp.zeros((bsz, H, Dh, Dh), jnp.float32)
    _, o = lax.scan(step, s0, (u, w, qg, qk, kd, g_last))
    return o.transpose(1, 0, 3, 2, 4).reshape(bsz, seq, H, Dh)


def gated_deltanet(h, w_in, conv_w, a_log, dt_bias, o_norm_g, w_out):
    bsz, seq, _ = h.shape
    H, Dh = GDN_HEADS, GDN_HEAD_DIM
    W = H * Dh
    proj = h @ w_in
    qkv = jax.nn.silu(_causal_dwconv(proj[..., :3 * W], conv_w))
    z = proj[..., 3 * W:4 * W].reshape(bsz, seq, H, Dh)
    a = proj[..., 4 * W:4 * W + H].astype(jnp.float32)
    b = proj[..., 4 * W + H:].astype(jnp.float32)
    q = _l2norm(qkv[..., :W].reshape(bsz, seq, H, Dh)) * (Dh ** -0.5)
    k = _l2norm(qkv[..., W:2 * W].reshape(bsz, seq, H, Dh))
    v = qkv[..., 2 * W:].reshape(bsz, seq, H, Dh).astype(jnp.float32)
    beta = jax.nn.sigmoid(b)
    g = -jnp.exp(a_log.astype(jnp.float32)) * jax.nn.softplus(a + dt_bias.astype(jnp.float32))
    o = _chunk_gated_delta(q, k, v, g, beta)
    o = _rms_norm(o, o_norm_g) * jax.nn.silu(z.astype(jnp.float32))
    return o.astype(h.dtype).reshape(bsz, seq, W) @ w_out


def forgetting_attention(h, w_in, b_f, q_norm_g, k_norm_g, w_out):
    bsz, seq, _ = h.shape
    H, Dh = FOX_HEADS, FOX_HEAD_DIM
    W = H * Dh
    proj = h @ w_in
    q = _rms_norm(proj[..., :W].reshape(bsz, seq, H, Dh), q_norm_g).transpose(0, 2, 1, 3)
    k = _rms_norm(proj[..., W:2 * W].reshape(bsz, seq, H, Dh), k_norm_g).transpose(0, 2, 1, 3)
    v = proj[..., 2 * W:3 * W].reshape(bsz, seq, H, Dh).transpose(0, 2, 1, 3)
    log_f = jax.nn.log_sigmoid(proj[..., 3 * W:].astype(jnp.float32) + b_f.astype(jnp.float32))
    c = jnp.cumsum(log_f, axis=1).transpose(0, 2, 1)
    scale = Dh ** -0.5
    q_idx = jnp.arange(FOX_BLOCK)
    outs = []
    for blk in range(seq // FOX_BLOCK):
        s0 = blk * FOX_BLOCK
        s1 = s0 + FOX_BLOCK
        logits = jnp.einsum("bhqd,bhkd->bhqk", q[:, :, s0:s1], k[:, :, :s1]).astype(jnp.float32) * scale
        logits = logits + c[:, :, s0:s1, None] - c[:, :, None, :s1]
        causal = jnp.arange(s1)[None, :] <= (s0 + q_idx)[:, None]
        p = jax.nn.softmax(jnp.where(causal, logits, -jnp.inf), axis=-1)
        outs.append(jnp.einsum("bhqk,bhkd->bhqd", p.astype(v.dtype), v[:, :, :s1]))
    o = jnp.concatenate(outs, axis=2).transpose(0, 2, 1, 3).reshape(bsz, seq, W)
    return o @ w_out


def conv_ffn(h, w_up, w_dw, w_down):
    u = _causal_dwconv(h @ w_up, w_dw)
    gate, up = jnp.split(u, 2, axis=-1)
    return (jax.nn.silu(gate) * up) @ w_down


def setup_inputs(seed: int = 0) -> dict:
    key = jax.random.key(seed)
    ks = iter(jax.random.split(key, 40))
    n_a = len(range(0, DEPTH, N_MIXERS))
    n_b = len(range(1, DEPTH, N_MIXERS))
    n_c = len(range(2, DEPTH, N_MIXERS))
    D = D_MODEL
    Wg = GDN_HEADS * GDN_HEAD_DIM
    Wf = FOX_HEADS * FOX_HEAD_DIM
    f32 = jnp.float32
    dense = lambda shape, fan_in: jax.random.normal(next(ks), shape, f32) * (fan_in ** -0.5)
    gain = lambda shape: 1.0 + 0.02 * jax.random.normal(next(ks), shape, f32)
    small = lambda shape: 0.02 * jax.random.normal(next(ks), shape, f32)

    x = jax.random.normal(next(ks), (BATCH, SEQ, D), f32)
    mix_norm_g = gain((DEPTH, D))
    ffn_norm_g = gain((DEPTH, D))

    conv_w_in = dense((n_a, D, 2 * D), D)
    conv_b_in = small((n_a, 2 * D))
    conv_w_dw = dense((n_a, CONF_KERNEL, D), CONF_KERNEL)
    conv_b_dw = small((n_a, D))
    conv_ln_g = gain((n_a, D))
    conv_ln_b = small((n_a, D))
    conv_w_out = dense((n_a, D, D), D)

    gdn_w_in = dense((n_b, D, 4 * Wg + 2 * GDN_HEADS), D)
    gdn_conv_w = dense((n_b, GDN_CONV, 3 * Wg), GDN_CONV)
    gdn_a_log = jnp.log(jax.random.uniform(next(ks), (n_b, GDN_HEADS), f32, 1.0, 16.0))
    dt = jnp.exp(jax.random.uniform(next(ks), (n_b, GDN_HEADS), f32, np.log(1e-3), np.log(1e-1)))
    gdn_dt_bias = dt + jnp.log(-jnp.expm1(-dt))
    gdn_o_norm_g = gain((n_b, GDN_HEAD_DIM))
    gdn_w_out = dense((n_b, Wg, D), Wg)

    fox_w_in = dense((n_c, D, 3 * Wf + FOX_HEADS), D)
    fox_b_f = 3.0 + 0.5 * jax.random.normal(next(ks), (n_c, FOX_HEADS), f32)
    fox_q_norm_g = gain((n_c, FOX_HEAD_DIM))
    fox_k_norm_g = gain((n_c, FOX_HEAD_DIM))
    fox_w_out = dense((n_c, Wf, D), Wf)

    ffn_w_up = dense((DEPTH, D, 2 * D_FF), D)
    ffn_w_dw = dense((DEPTH, FFN_CONV, 2 * D_FF), FFN_CONV)
    ffn_w_down = dense((DEPTH, D_FF, D), D_FF)

    return {"x": x, "mix_norm_g": mix_norm_g, "ffn_norm_g": ffn_norm_g,
            "conv_w_in": conv_w_in, "conv_b_in": conv_b_in, "conv_w_dw": conv_w_dw, "conv_b_dw": conv_b_dw,
            "conv_ln_g": conv_ln_g, "conv_ln_b": conv_ln_b, "conv_w_out": conv_w_out,
            "gdn_w_in": gdn_w_in, "gdn_conv_w": gdn_conv_w, "gdn_a_log": gdn_a_log, "gdn_dt_bias": gdn_dt_bias,
            "gdn_o_norm_g": gdn_o_norm_g, "gdn_w_out": gdn_w_out,
            "fox_w_in": fox_w_in, "fox_b_f": fox_b_f, "fox_q_norm_g": fox_q_norm_g, "fox_k_norm_g": fox_k_norm_g,
            "fox_w_out": fox_w_out,
            "ffn_w_up": ffn_w_up, "ffn_w_dw": ffn_w_dw, "ffn_w_down": ffn_w_down}


def reference(x, mix_norm_g, ffn_norm_g,
              conv_w_in, conv_b_in, conv_w_dw, conv_b_dw, conv_ln_g, conv_ln_b, conv_w_out,
              gdn_w_in, gdn_conv_w, gdn_a_log, gdn_dt_bias, gdn_o_norm_g, gdn_w_out,
              fox_w_in, fox_b_f, fox_q_norm_g, fox_k_norm_g, fox_w_out,
              ffn_w_up, ffn_w_dw, ffn_w_down):
    ia = ib = ic = 0
    for layer in range(DEPTH):
        h = _rms_norm(x, mix_norm_g[layer])
        kind = layer % N_MIXERS
        if kind == 0:
            y = conformer_conv(h, conv_w_in[ia], conv_b_in[ia], conv_w_dw[ia], conv_b_dw[ia],
                               conv_ln_g[ia], conv_ln_b[ia], conv_w_out[ia])
            ia += 1
        elif kind == 1:
            y = gated_deltanet(h, gdn_w_in[ib], gdn_conv_w[ib], gdn_a_log[ib], gdn_dt_bias[ib],
                               gdn_o_norm_g[ib], gdn_w_out[ib])
            ib += 1
        else:
            y = forgetting_attention(h, fox_w_in[ic], fox_b_f[ic], fox_q_norm_g[ic], fox_k_norm_g[ic],
                                     fox_w_out[ic])
            ic += 1
        x = x + y.astype(x.dtype)
        h = _rms_norm(x, ffn_norm_g[layer])
        x = x + conv_ffn(h, ffn_w_up[layer], ffn_w_dw[layer], ffn_w_down[layer]).astype(x.dtype)
    return x
```

```python
import functools

import jax
import jax.numpy as jnp
from jax import lax
from jax.experimental import pallas as pl
from jax.experimental.pallas import tpu as pltpu

EPS = 1e-6
N_MIXERS = 3
HEAD_DIM = 128
GDN_CHUNK = 64
LANES = 128
SUBLANES = 8
VMEM_LIMIT = 56 * 1024 * 1024

_BF16 = jnp.bfloat16
_F32 = jnp.float32


def _dot(a, b):
    return jnp.dot(a.astype(_BF16), b.astype(_BF16), preferred_element_type=_F32)


def _dot_nt(a, b):
    return lax.dot_general(a.astype(_BF16), b.astype(_BF16), (((1,), (1,)), ((), ())),
                           preferred_element_type=_F32)


def _dot_tn(a, b):
    return lax.dot_general(a.astype(_BF16), b.astype(_BF16), (((0,), (0,)), ((), ())),
                           preferred_element_type=_F32)


def _dot_f32(a, b):
    return jnp.dot(a, b, preferred_element_type=_F32, precision=lax.Precision.HIGHEST)


def _dot_nt_f32(a, b):
    return lax.dot_general(a, b, (((1,), (1,)), ((), ())), preferred_element_type=_F32,
                           precision=lax.Precision.HIGHEST)


def _rms_rows(x, g):
    return x * lax.rsqrt(jnp.mean(x * x, axis=-1, keepdims=True) + EPS) * g


def _silu(x):
    return x * jax.nn.sigmoid(x)


def _resident(shape):
    nd = len(shape)
    return pl.BlockSpec(shape, lambda *_: (0,) * nd, pipeline_mode=pl.Buffered(1))


def _params(n_axes=2):
    sem = ("parallel",) + ("arbitrary",) * (n_axes - 1)
    return pltpu.CompilerParams(dimension_semantics=sem, vmem_limit_bytes=VMEM_LIMIT)


def _ffn_kernel(x_ref, g_ref, wup_ref, wdw_ref, wdown_ref, o_ref, carry_ref, ubuf_ref, *,
                tm, d_ff, cw):
    t = pl.program_id(1)

    @pl.when(t == 0)
    def _():
        carry_ref[...] = jnp.zeros_like(carry_ref)

    x = x_ref[...]
    h = _rms_rows(x, g_ref[...]).astype(_BF16)
    acc = x
    taps = wdw_ref.shape[0]
    for c in range(d_ff // cw):
        halves = []
        for col in (c * cw, d_ff + c * cw):
            u = jnp.dot(h, wup_ref[:, col:col + cw], preferred_element_type=_F32)
            ubuf_ref[0:SUBLANES, :] = carry_ref[:, col:col + cw]
            ubuf_ref[SUBLANES:SUBLANES + tm, :] = u
            carry_ref[:, col:col + cw] = u[tm - SUBLANES:, :]
            y = u * wdw_ref[taps - 1:taps, col:col + cw]
            for k in range(taps - 1):
                off = SUBLANES - (taps - 1) + k
                y = y + ubuf_ref[off:off + tm, :] * wdw_ref[k:k + 1, col:col + cw]
            halves.append(y)
        act = (_silu(halves[0]) * halves[1]).astype(_BF16)
        acc = acc + jnp.dot(act, wdown_ref[c * cw:(c + 1) * cw, :], preferred_element_type=_F32)
    o_ref[...] = acc


def _conv_ffn(x, g, w_up, w_dw, w_down, *, batch, tm=512, cw=256):
    m, d = x.shape
    seq = m // batch
    d_ff = w_down.shape[0]
    nt = seq // tm
    row = lambda b, t: (b * nt + t, 0)
    return pl.pallas_call(
        functools.partial(_ffn_kernel, tm=tm, d_ff=d_ff, cw=cw),
        grid=(batch, nt),
        in_specs=[pl.BlockSpec((tm, d), row), _resident((1, d)), _resident(w_up.shape),
                  _resident(w_dw.shape), _resident(w_down.shape)],
        out_specs=pl.BlockSpec((tm, d), row),
        out_shape=jax.ShapeDtypeStruct((m, d), _F32),
        scratch_shapes=[pltpu.VMEM((SUBLANES, 2 * d_ff), _F32),
                        pltpu.VMEM((tm + SUBLANES, cw), _F32)],
        compiler_params=_params(),
        name="conv_ffn",
    )(x, g.reshape(1, d), w_up.astype(_BF16), w_dw, w_down.astype(_BF16))


def _conformer_kernel(x_ref, g_ref, win_ref, bin_ref, wdw_ref, bdw_ref, lng_ref, lnb_ref, wout_ref,
                      o_ref, ubuf_ref, *, tm, halo):
    t = pl.program_id(1)
    d = x_ref.shape[1]

    @pl.when(t == 0)
    def _():
        ubuf_ref[0:halo, :] = jnp.zeros((halo, d), _F32)

    x = x_ref[...]
    h = _rms_rows(x, g_ref[...]).astype(_BF16)
    val = jnp.dot(h, win_ref[:, 0:d], preferred_element_type=_F32) + bin_ref[:, 0:d]
    gate = jnp.dot(h, win_ref[:, d:2 * d], preferred_element_type=_F32) + bin_ref[:, d:2 * d]
    ubuf_ref[halo:halo + tm, :] = val * jax.nn.sigmoid(gate)

    taps = wdw_ref.shape[0]
    y = jnp.zeros((tm, d), _F32) + bdw_ref[...]
    for k in range(taps):
        off = halo - (taps - 1) + k
        y = y + ubuf_ref[off:off + tm, :] * wdw_ref[k:k + 1, :]
    ubuf_ref[0:halo, :] = ubuf_ref[tm:tm + halo, :]

    yc = y - jnp.mean(y, axis=-1, keepdims=True)
    var = jnp.mean(yc * yc, axis=-1, keepdims=True)
    z = _silu(yc * lax.rsqrt(var + EPS) * lng_ref[...] + lnb_ref[...])
    o_ref[...] = x + jnp.dot(z.astype(_BF16), wout_ref[...], preferred_element_type=_F32)


def _conformer(x, g, w_in, b_in, w_dw, b_dw, ln_g, ln_b, w_out, *, batch, tm=256):
    m, d = x.shape
    seq = m // batch
    nt = seq // tm
    taps = w_dw.shape[0]
    halo = -(-(taps - 1) // SUBLANES) * SUBLANES
    row = lambda b, t: (b * nt + t, 0)
    return pl.pallas_call(
        functools.partial(_conformer_kernel, tm=tm, halo=halo),
        grid=(batch, nt),
        in_specs=[pl.BlockSpec((tm, d), row), _resident((1, d)), _resident(w_in.shape),
                  _resident((1, 2 * d)), _resident(w_dw.shape), _resident((1, d)),
                  _resident((1, d)), _resident((1, d)), _resident(w_out.shape)],
        out_specs=pl.BlockSpec((tm, d), row),
        out_shape=jax.ShapeDtypeStruct((m, d), _F32),
        scratch_shapes=[pltpu.VMEM((tm + halo, d), _F32)],
        compiler_params=_params(),
        name="conformer",
    )(x, g.reshape(1, d), w_in.astype(_BF16), b_in.reshape(1, 2 * d), w_dw, b_dw.reshape(1, d),
      ln_g.reshape(1, d), ln_b.reshape(1, d), w_out.astype(_BF16))


def _out_proj_kernel(a_ref, w_ref, res_ref, o_ref):
    o_ref[...] = res_ref[...] + jnp.dot(a_ref[...], w_ref[...], preferred_element_type=_F32)


def _out_proj(a, w, res, *, tm=1024):
    m, k = a.shape
    d = w.shape[1]
    return pl.pallas_call(
        _out_proj_kernel,
        grid=(m // tm,),
        in_specs=[pl.BlockSpec((tm, k), lambda i: (i, 0)), _resident(w.shape),
                  pl.BlockSpec((tm, d), lambda i: (i, 0))],
        out_specs=pl.BlockSpec((tm, d), lambda i: (i, 0)),
        out_shape=jax.ShapeDtypeStruct((m, d), _F32),
        compiler_params=_params(1),
        name="out_proj",
    )(a, w.astype(_BF16), res)


def _gdn_in_kernel(x_ref, g_ref, w_ref, wab_ref, cw_ref, qkv_ref, z_ref, ab_ref, carry_ref, ubuf_ref,
                   *, tm, cwid, n_conv):
    t = pl.program_id(1)

    @pl.when(t == 0)
    def _():
        carry_ref[...] = jnp.zeros_like(carry_ref)

    h = _rms_rows(x_ref[...], g_ref[...]).astype(_BF16)
    taps = cw_ref.shape[0]
    for c in range(n_conv // cwid):
        col = c * cwid
        u = jnp.dot(h, w_ref[:, col:col + cwid], preferred_element_type=_F32)
        ubuf_ref[0:SUBLANES, :] = carry_ref[:, col:col + cwid]
        ubuf_ref[SUBLANES:SUBLANES + tm, :] = u
        carry_ref[:, col:col + cwid] = u[tm - SUBLANES:, :]
        y = u * cw_ref[taps - 1:taps, col:col + cwid]
        for k in range(taps - 1):
            off = SUBLANES - (taps - 1) + k
            y = y + ubuf_ref[off:off + tm, :] * cw_ref[k:k + 1, col:col + cwid]
        qkv_ref[:, col:col + cwid] = _silu(y)
    z_ref[...] = jnp.dot(h, w_ref[:, n_conv:], preferred_element_type=_F32)
    ab_ref[...] = jnp.dot(h, wab_ref[...], preferred_element_type=_F32)


def _gdn_in(x, g, w_in, conv_w, *, batch, tm=512, cwid=512):
    m, d = x.shape
    seq = m // batch
    nt = seq // tm
    n_conv = conv_w.shape[1]
    wz = w_in.shape[1] - n_conv
    w_main = w_in[:, :n_conv + d].astype(_BF16)
    n_ab = w_in.shape[1] - (n_conv + d)
    w_ab = jnp.pad(w_in[:, n_conv + d:], ((0, 0), (0, LANES - n_ab))).astype(_BF16)
    del wz
    row = lambda b, t: (b * nt + t, 0)
    return pl.pallas_call(
        functools.partial(_gdn_in_kernel, tm=tm, cwid=cwid, n_conv=n_conv),
        grid=(batch, nt),
        in_specs=[pl.BlockSpec((tm, d), row), _resident((1, d)), _resident(w_main.shape),
                  _resident(w_ab.shape), _resident(conv_w.shape)],
        out_specs=[pl.BlockSpec((tm, n_conv), row), pl.BlockSpec((tm, d), row),
                   pl.BlockSpec((tm, LANES), row)],
        out_shape=[jax.ShapeDtypeStruct((m, n_conv), _F32), jax.ShapeDtypeStruct((m, d), _F32),
                   jax.ShapeDtypeStruct((m, LANES), _F32)],
        scratch_shapes=[pltpu.VMEM((SUBLANES, n_conv), _F32),
                        pltpu.VMEM((tm + SUBLANES, cwid), _F32)],
        compiler_params=_params(),
        name="gdn_in",
    )(x, g.reshape(1, d), w_main, w_ab, conv_w)


def _gdn_chunk_kernel(qkv_ref, z_ref, ab_ref, alog_ref, dtb_ref, ong_ref, o_ref, state_ref, *,
                      ts, heads):
    t = pl.program_id(1)
    C, Dh = GDN_CHUNK, HEAD_DIM
    W = heads * Dh

    @pl.when(t == 0)
    def _():
        state_ref[...] = jnp.zeros_like(state_ref)

    ri = lax.broadcasted_iota(jnp.int32, (C, C), 0)
    ci = lax.broadcasted_iota(jnp.int32, (C, C), 1)
    lower = ri >= ci
    strict = ri > ci
    eye = (ri == ci).astype(_F32)
    tri = lower.astype(_F32)
    sel = (lax.broadcasted_iota(jnp.int32, (SUBLANES, LANES), 0)
           == lax.broadcasted_iota(jnp.int32, (SUBLANES, LANES), 1)).astype(_F32)
    neg_a = -jnp.exp(alog_ref[...])
    dtb = dtb_ref[...]
    ong = ong_ref[...]

    def chunk(n, carry):
        r0 = pl.multiple_of(n * C, C)
        ab = ab_ref[pl.ds(r0, C), :]
        g_raw = neg_a * jax.nn.softplus(ab + dtb)
        beta_all = jax.nn.sigmoid(ab)
        gc = _dot_f32(tri, g_raw)
        gc_rows = _dot_nt_f32(sel, gc)
        for hd in range(heads):
            lo = hd * Dh
            q = qkv_ref[pl.ds(r0, C), lo:lo + Dh]
            k = qkv_ref[pl.ds(r0, C), W + lo:W + lo + Dh]
            v = qkv_ref[pl.ds(r0, C), 2 * W + lo:2 * W + lo + Dh]
            q = q * lax.rsqrt(jnp.sum(q * q, axis=-1, keepdims=True) + EPS) * (Dh ** -0.5)
            k = k * lax.rsqrt(jnp.sum(k * k, axis=-1, keepdims=True) + EPS)
            gcol = gc[:, hd:hd + 1]
            grow = gc_rows[hd:hd + 1, :]
            glast = gcol[C - 1:C, :]
            beta = beta_all[:, heads + hd:heads + hd + 1]
            kb = k * beta
            vb = v * beta
            decay = jnp.where(lower, jnp.exp(jnp.where(lower, gcol - grow, 0.0)), 0.0)
            a_mat = jnp.where(strict, _dot_nt_f32(kb, k) * decay, 0.0)
            t_mat = eye - a_mat
            pw = a_mat
            for _ in range(5):
                pw = _dot_f32(pw, pw)
                t_mat = t_mat + _dot_f32(t_mat, pw)
            eg = jnp.exp(gcol)
            u = _dot_f32(t_mat, vb)
            w = _dot_f32(t_mat, kb * eg)
            qk = jnp.where(lower, _dot_nt(q, k) * decay, 0.0)
            s = state_ref[hd]
            v_new = u - _dot(w, s)
            o = _dot(q * eg, s) + _dot(qk, v_new)
            state_ref[hd] = s * jnp.exp(glast) + _dot_tn(k * jnp.exp(glast - gcol), v_new)
            o = _rms_rows(o, ong) * _silu(z_ref[pl.ds(r0, C), lo:lo + Dh])
            o_ref[pl.ds(r0, C), lo:lo + Dh] = o.astype(o_ref.dtype)
        return carry

    lax.fori_loop(0, ts // C, chunk, 0)


def _gdn_chunk(qkv, z, ab, a_log, dt_bias, o_norm_g, *, batch, heads, ts=256):
    m, w3 = qkv.shape
    d = z.shape[1]
    seq = m // batch
    nt = seq // ts
    row = lambda b, t: (b * nt + t, 0)
    pad = lambda v: jnp.pad(v.reshape(1, -1), ((0, 0), (0, LANES - v.shape[0])))
    return pl.pallas_call(
        functools.partial(_gdn_chunk_kernel, ts=ts, heads=heads),
        grid=(batch, nt),
        in_specs=[pl.BlockSpec((ts, w3), row), pl.BlockSpec((ts, d), row),
                  pl.BlockSpec((ts, LANES), row), _resident((1, LANES)), _resident((1, LANES)),
                  _resident((1, HEAD_DIM))],
        out_specs=pl.BlockSpec((ts, d), row),
        out_shape=jax.ShapeDtypeStruct((m, d), _BF16),
        scratch_shapes=[pltpu.VMEM((heads, HEAD_DIM, HEAD_DIM), _F32)],
        compiler_params=_params(),
        name="gdn_chunk",
    )(qkv, z, ab, pad(a_log), pad(dt_bias), o_norm_g.reshape(1, HEAD_DIM))


def _gated_deltanet(x, g, w_in, conv_w, a_log, dt_bias, o_norm_g, w_out, *, batch):
    heads = a_log.shape[0]
    qkv, z, ab = _gdn_in(x, g, w_in, conv_w, batch=batch)
    o = _gdn_chunk(qkv, z, ab, a_log, dt_bias, o_norm_g, batch=batch, heads=heads)
    return _out_proj(o, w_out, x)


def _fox_in_kernel(x_ref, g_ref, w_ref, wf_ref, bf_ref, qg_ref, kg_ref, q_ref, k_ref, v_ref, c_ref,
                   csum_ref, *, tm, heads):
    t = pl.program_id(1)
    Dh = HEAD_DIM
    W = heads * Dh

    @pl.when(t == 0)
    def _():
        csum_ref[...] = jnp.zeros_like(csum_ref)

    h = _rms_rows(x_ref[...], g_ref[...]).astype(_BF16)
    scale = Dh ** -0.5
    for hd in range(heads):
        lo = hd * Dh
        q = jnp.dot(h, w_ref[:, lo:lo + Dh], preferred_element_type=_F32)
        q_ref[:, lo:lo + Dh] = (_rms_rows(q, qg_ref[...]) * scale).astype(q_ref.dtype)
        k = jnp.dot(h, w_ref[:, W + lo:W + lo + Dh], preferred_element_type=_F32)
        k_ref[:, lo:lo + Dh] = _rms_rows(k, kg_ref[...]).astype(k_ref.dtype)
    v_ref[...] = jnp.dot(h, w_ref[:, 2 * W:3 * W], preferred_element_type=_F32).astype(v_ref.dtype)

    f = jnp.dot(h, wf_ref[...], preferred_element_type=_F32) + bf_ref[...]
    log_f = jax.nn.log_sigmoid(f)
    ri = lax.broadcasted_iota(jnp.int32, (tm, tm), 0)
    ci = lax.broadcasted_iota(jnp.int32, (tm, tm), 1)
    c = _dot_f32((ri >= ci).astype(_F32), log_f) + csum_ref[0:1, :]
    c_ref[...] = c
    csum_ref[0:1, :] = c[tm - 1:tm, :]


def _fox_in(x, g, w_in, b_f, q_norm_g, k_norm_g, *, batch, heads, tm=256):
    m, d = x.shape
    seq = m // batch
    nt = seq // tm
    W = heads * HEAD_DIM
    w_main = w_in[:, :3 * W].astype(_BF16)
    w_f = jnp.pad(w_in[:, 3 * W:], ((0, 0), (0, LANES - heads))).astype(_BF16)
    b_pad = jnp.pad(b_f.reshape(1, heads), ((0, 0), (0, LANES - heads)))
    row = lambda b, t: (b * nt + t, 0)
    return pl.pallas_call(
        functools.partial(_fox_in_kernel, tm=tm, heads=heads),
        grid=(batch, nt),
        in_specs=[pl.BlockSpec((tm, d), row), _resident((1, d)), _resident(w_main.shape),
                  _resident(w_f.shape), _resident((1, LANES)), _resident((1, HEAD_DIM)),
                  _resident((1, HEAD_DIM))],
        out_specs=[pl.BlockSpec((tm, W), row)] * 3 + [pl.BlockSpec((tm, LANES), row)],
        out_shape=[jax.ShapeDtypeStruct((m, W), _BF16)] * 3 + [jax.ShapeDtypeStruct((m, LANES), _F32)],
        scratch_shapes=[pltpu.VMEM((SUBLANES, LANES), _F32)],
        compiler_params=_params(),
        name="fox_in",
    )(x, g.reshape(1, d), w_main, w_f, b_pad, q_norm_g.reshape(1, HEAD_DIM),
      k_norm_g.reshape(1, HEAD_DIM))


def _fox_attn_kernel(q_ref, k_ref, v_ref, ccol_ref, crow_ref, o_ref, *, tq, heads):
    qi = pl.program_id(1)
    Dh = HEAD_DIM
    ri = lax.broadcasted_iota(jnp.int32, (tq, tq), 0)
    ci = lax.broadcasted_iota(jnp.int32, (tq, tq), 1)
    causal = ci <= ri

    for hd in range(heads):
        lo = hd * Dh
        q = q_ref[:, lo:lo + Dh]
        cq = ccol_ref[:, hd:hd + 1]

        def logits(j):
            k0 = pl.multiple_of(j * tq, tq)
            s = _dot_nt(q, k_ref[pl.ds(k0, tq), lo:lo + Dh])
            return s + cq - crow_ref[hd, pl.ds(j, 1), :], k0

        def update(s, k0, m_run, l_run, acc):
            m_new = jnp.maximum(m_run, jnp.max(s, axis=-1, keepdims=True))
            alpha = jnp.exp(m_run - m_new)
            p = jnp.exp(s - m_new)
            l_new = alpha * l_run + jnp.sum(p, axis=-1, keepdims=True)
            acc = alpha * acc + _dot(p, v_ref[pl.ds(k0, tq), lo:lo + Dh])
            return m_new, l_new, acc

        def body(j, carry):
            s, k0 = logits(j)
            return update(s, k0, *carry)

        init = (jnp.full((tq, 1), -jnp.inf, _F32), jnp.zeros((tq, 1), _F32),
                jnp.zeros((tq, Dh), _F32))
        carry = lax.fori_loop(0, qi, body, init)
        s, k0 = logits(qi)
        s = jnp.where(causal, s, -jnp.inf)
        _, l_fin, acc = update(s, k0, *carry)
        o_ref[:, lo:lo + Dh] = (acc / l_fin).astype(o_ref.dtype)


def _fox_attn(q, k, v, c, *, batch, heads, tq=256):
    m, W = q.shape
    seq = m // batch
    nq = seq // tq
    c_rows = c[:, :heads].reshape(batch, seq, heads).transpose(0, 2, 1).reshape(batch, heads, nq, tq)
    row = lambda b, t: (b * nq + t, 0)
    return pl.pallas_call(
        functools.partial(_fox_attn_kernel, tq=tq, heads=heads),
        grid=(batch, nq),
        in_specs=[pl.BlockSpec((tq, W), row), pl.BlockSpec((seq, W), lambda b, t: (b, 0)),
                  pl.BlockSpec((seq, W), lambda b, t: (b, 0)), pl.BlockSpec((tq, LANES), row),
                  pl.BlockSpec((None, heads, nq, tq), lambda b, t: (b, 0, 0, 0))],
        out_specs=pl.BlockSpec((tq, W), row),
        out_shape=jax.ShapeDtypeStruct((m, W), _BF16),
        compiler_params=_params(),
        name="fox_attn",
    )(q, k, v, c, c_rows)


def _forgetting_attention(x, g, w_in, b_f, q_norm_g, k_norm_g, w_out, *, batch):
    heads = b_f.shape[0]
    q, k, v, c = _fox_in(x, g, w_in, b_f, q_norm_g, k_norm_g, batch=batch, heads=heads)
    o = _fox_attn(q, k, v, c, batch=batch, heads=heads)
    return _out_proj(o, w_out, x)


def kernel(x, mix_norm_g, ffn_norm_g, conv_w_in, conv_b_in, conv_w_dw, conv_b_dw, conv_ln_g, conv_ln_b,
           conv_w_out, gdn_w_in, gdn_conv_w, gdn_a_log, gdn_dt_bias, gdn_o_norm_g, gdn_w_out, fox_w_in,
           fox_b_f, fox_q_norm_g, fox_k_norm_g, fox_w_out, ffn_w_up, ffn_w_dw, ffn_w_down):
    batch, seq, d = x.shape
    depth = mix_norm_g.shape[0]
    xs = x.reshape(batch * seq, d)
    ia = ib = ic = 0
    for layer in range(depth):
        kind = layer % N_MIXERS
        g = mix_norm_g[layer]
        if kind == 0:
            xs = _conformer(xs, g, conv_w_in[ia], conv_b_in[ia], conv_w_dw[ia], conv_b_dw[ia],
                            conv_ln_g[ia], conv_ln_b[ia], conv_w_out[ia], batch=batch)
            ia += 1
        elif kind == 1:
            xs = _gated_deltanet(xs, g, gdn_w_in[ib], gdn_conv_w[ib], gdn_a_log[ib], gdn_dt_bias[ib],
                                 gdn_o_norm_g[ib], gdn_w_out[ib], batch=batch)
            ib += 1
        else:
            xs = _forgetting_attention(xs, g, fox_w_in[ic], fox_b_f[ic], fox_q_norm_g[ic],
                                       fox_k_norm_g[ic], fox_w_out[ic], batch=batch)
            ic += 1
        xs = _conv_ffn(xs, ffn_norm_g[layer], ffn_w_up[layer], ffn_w_dw[layer], ffn_w_down[layer],
                       batch=batch)
    return xs.reshape(batch, seq, d)
```

```python
import functools

import jax
import jax.numpy as jnp
from jax import lax
from jax.experimental import pallas as pl
from jax.experimental.pallas import tpu as pltpu

EPS = 1e-6
N_MIXERS = 3
HEAD_DIM = 128
GDN_CHUNK = 64
LANES = 128
SUBLANES = 8
VMEM_LIMIT = 56 * 1024 * 1024

_BF16 = jnp.bfloat16
_F32 = jnp.float32


def _dot(a, b):
    return jnp.dot(a.astype(_BF16), b.astype(_BF16), preferred_element_type=_F32)


def _dot_nt(a, b):
    return lax.dot_general(a.astype(_BF16), b.astype(_BF16), (((1,), (1,)), ((), ())),
                           preferred_element_type=_F32)


def _dot_tn(a, b):
    return lax.dot_general(a.astype(_BF16), b.astype(_BF16), (((0,), (0,)), ((), ())),
                           preferred_element_type=_F32)


def _dot_f32(a, b):
    return jnp.dot(a, b, preferred_element_type=_F32, precision=lax.Precision.HIGHEST)


def _dot_nt_f32(a, b):
    return lax.dot_general(a, b, (((1,), (1,)), ((), ())), preferred_element_type=_F32,
                           precision=lax.Precision.HIGHEST)


def _rms_rows(x, g):
    return x * lax.rsqrt(jnp.mean(x * x, axis=-1, keepdims=True) + EPS) * g


def _silu(x):
    return x * jax.nn.sigmoid(x)


def _resident(shape):
    nd = len(shape)
    return pl.BlockSpec(shape, lambda *_: (0,) * nd, pipeline_mode=pl.Buffered(1))


def _params(n_axes=2):
    sem = ("parallel",) + ("arbitrary",) * (n_axes - 1)
    return pltpu.CompilerParams(dimension_semantics=sem, vmem_limit_bytes=VMEM_LIMIT)


def _ffn_kernel(x_ref, g_ref, wup_ref, wdw_ref, wdown_ref, o_ref, carry_ref, ubuf_ref, *,
                tm, d_ff, cw):
    t = pl.program_id(1)

    @pl.when(t == 0)
    def _():
        carry_ref[...] = jnp.zeros_like(carry_ref)

    x = x_ref[...]
    h = _rms_rows(x, g_ref[...]).astype(_BF16)
    acc = x
    taps = wdw_ref.shape[0]
    for c in range(d_ff // cw):
        halves = []
        for col in (c * cw, d_ff + c * cw):
            u = jnp.dot(h, wup_ref[:, col:col + cw], preferred_element_type=_F32)
            ubuf_ref[0:SUBLANES, :] = carry_ref[:, col:col + cw]
            ubuf_ref[SUBLANES:SUBLANES + tm, :] = u
            carry_ref[:, col:col + cw] = u[tm - SUBLANES:, :]
            y = u * wdw_ref[taps - 1:taps, col:col + cw]
            for k in range(taps - 1):
                off = SUBLANES - (taps - 1) + k
                y = y + ubuf_ref[off:off + tm, :] * wdw_ref[k:k + 1, col:col + cw]
            halves.append(y)
        act = (_silu(halves[0]) * halves[1]).astype(_BF16)
        acc = acc + jnp.dot(act, wdown_ref[c * cw:(c + 1) * cw, :], preferred_element_type=_F32)
    o_ref[...] = acc


def _conv_ffn(x, g, w_up, w_dw, w_down, *, batch, tm=512, cw=256):
    m, d = x.shape
    seq = m // batch
    d_ff = w_down.shape[0]
    nt = seq // tm
    row = lambda b, t: (b * nt + t, 0)
    return pl.pallas_call(
        functools.partial(_ffn_kernel, tm=tm, d_ff=d_ff, cw=cw),
        grid=(batch, nt),
        in_specs=[pl.BlockSpec((tm, d), row), _resident((1, d)), _resident(w_up.shape),
                  _resident(w_dw.shape), _resident(w_down.shape)],
        out_specs=pl.BlockSpec((tm, d), row),
        out_shape=jax.ShapeDtypeStruct((m, d), _F32),
        scratch_shapes=[pltpu.VMEM((SUBLANES, 2 * d_ff), _F32),
                        pltpu.VMEM((tm + SUBLANES, cw), _F32)],
        compiler_params=_params(),
        name="conv_ffn",
    )(x, g.reshape(1, d), w_up.astype(_BF16), w_dw, w_down.astype(_BF16))


def _conformer_kernel(x_ref, g_ref, win_ref, bin_ref, wdw_ref, bdw_ref, lng_ref, lnb_ref, wout_ref,
                      o_ref, ubuf_ref, *, tm, halo):
    t = pl.program_id(1)
    d = x_ref.shape[1]

    @pl.when(t == 0)
    def _():
        ubuf_ref[0:halo, :] = jnp.zeros((halo, d), _F32)

    x = x_ref[...]
    h = _rms_rows(x, g_ref[...]).astype(_BF16)
    val = jnp.dot(h, win_ref[:, 0:d], preferred_element_type=_F32) + bin_ref[:, 0:d]
    gate = jnp.dot(h, win_ref[:, d:2 * d], preferred_element_type=_F32) + bin_ref[:, d:2 * d]
    ubuf_ref[halo:halo + tm, :] = val * jax.nn.sigmoid(gate)

    taps = wdw_ref.shape[0]
    y = jnp.zeros((tm, d), _F32) + bdw_ref[...]
    for k in range(taps):
        off = halo - (taps - 1) + k
        y = y + ubuf_ref[off:off + tm, :] * wdw_ref[k:k + 1, :]
    ubuf_ref[0:halo, :] = ubuf_ref[tm:tm + halo, :]

    yc = y - jnp.mean(y, axis=-1, keepdims=True)
    var = jnp.mean(yc * yc, axis=-1, keepdims=True)
    z = _silu(yc * lax.rsqrt(var + EPS) * lng_ref[...] + lnb_ref[...])
    o_ref[...] = x + jnp.dot(z.astype(_BF16), wout_ref[...], preferred_element_type=_F32)


def _conformer(x, g, w_in, b_in, w_dw, b_dw, ln_g, ln_b, w_out, *, batch, tm=256):
    m, d = x.shape
    seq = m // batch
    nt = seq // tm
    taps = w_dw.shape[0]
    halo = -(-(taps - 1) // SUBLANES) * SUBLANES
    row = lambda b, t: (b * nt + t, 0)
    return pl.pallas_call(
        functools.partial(_conformer_kernel, tm=tm, halo=halo),
        grid=(batch, nt),
        in_specs=[pl.BlockSpec((tm, d), row), _resident((1, d)), _resident(w_in.shape),
                  _resident((1, 2 * d)), _resident(w_dw.shape), _resident((1, d)),
                  _resident((1, d)), _resident((1, d)), _resident(w_out.shape)],
        out_specs=pl.BlockSpec((tm, d), row),
        out_shape=jax.ShapeDtypeStruct((m, d), _F32),
        scratch_shapes=[pltpu.VMEM((tm + halo, d), _F32)],
        compiler_params=_params(),
        name="conformer",
    )(x, g.reshape(1, d), w_in.astype(_BF16), b_in.reshape(1, 2 * d), w_dw, b_dw.reshape(1, d),
      ln_g.reshape(1, d), ln_b.reshape(1, d), w_out.astype(_BF16))


def _out_proj_kernel(a_ref, w_ref, res_ref, o_ref):
    o_ref[...] = res_ref[...] + jnp.dot(a_ref[...], w_ref[...], preferred_element_type=_F32)


def _out_proj(a, w, res, *, tm=1024):
    m, k = a.shape
    d = w.shape[1]
    return pl.pallas_call(
        _out_proj_kernel,
        grid=(m // tm,),
        in_specs=[pl.BlockSpec((tm, k), lambda i: (i, 0)), _resident(w.shape),
                  pl.BlockSpec((tm, d), lambda i: (i, 0))],
        out_specs=pl.BlockSpec((tm, d), lambda i: (i, 0)),
        out_shape=jax.ShapeDtypeStruct((m, d), _F32),
        compiler_params=_params(1),
        name="out_proj",
    )(a, w.astype(_BF16), res)


def _gdn_in_kernel(x_ref, g_ref, w_ref, wab_ref, cw_ref, qkv_ref, z_ref, ab_ref, carry_ref, ubuf_ref,
                   *, tm, cwid, n_conv):
    t = pl.program_id(1)

    @pl.when(t == 0)
    def _():
        carry_ref[...] = jnp.zeros_like(carry_ref)

    h = _rms_rows(x_ref[...], g_ref[...]).astype(_BF16)
    taps = cw_ref.shape[0]
    for c in range(n_conv // cwid):
        col = c * cwid
        u = jnp.dot(h, w_ref[:, col:col + cwid], preferred_element_type=_F32)
        ubuf_ref[0:SUBLANES, :] = carry_ref[:, col:col + cwid]
        ubuf_ref[SUBLANES:SUBLANES + tm, :] = u
        carry_ref[:, col:col + cwid] = u[tm - SUBLANES:, :]
        y = u * cw_ref[taps - 1:taps, col:col + cwid]
        for k in range(taps - 1):
            off = SUBLANES - (taps - 1) + k
            y = y + ubuf_ref[off:off + tm, :] * cw_ref[k:k + 1, col:col + cwid]
        qkv_ref[:, col:col + cwid] = _silu(y)
    z_ref[...] = jnp.dot(h, w_ref[:, n_conv:], preferred_element_type=_F32)
    ab_ref[...] = jnp.dot(h, wab_ref[...], preferred_element_type=_F32)


def _gdn_in(x, g, w_in, conv_w, *, batch, tm=512, cwid=512):
    m, d = x.shape
    seq = m // batch
    nt = seq // tm
    n_conv = conv_w.shape[1]
    wz = w_in.shape[1] - n_conv
    w_main = w_in[:, :n_conv + d].astype(_BF16)
    n_ab = w_in.shape[1] - (n_conv + d)
    w_ab = jnp.pad(w_in[:, n_conv + d:], ((0, 0), (0, LANES - n_ab))).astype(_BF16)
    del wz
    row = lambda b, t: (b * nt + t, 0)
    return pl.pallas_call(
        functools.partial(_gdn_in_kernel, tm=tm, cwid=cwid, n_conv=n_conv),
        grid=(batch, nt),
        in_specs=[pl.BlockSpec((tm, d), row), _resident((1, d)), _resident(w_main.shape),
                  _resident(w_ab.shape), _resident(conv_w.shape)],
        out_specs=[pl.BlockSpec((tm, n_conv), row), pl.BlockSpec((tm, d), row),
                   pl.BlockSpec((tm, LANES), row)],
        out_shape=[jax.ShapeDtypeStruct((m, n_conv), _F32), jax.ShapeDtypeStruct((m, d), _F32),
                   jax.ShapeDtypeStruct((m, LANES), _F32)],
        scratch_shapes=[pltpu.VMEM((SUBLANES, n_conv), _F32),
                        pltpu.VMEM((tm + SUBLANES, cwid), _F32)],
        compiler_params=_params(),
        name="gdn_in",
    )(x, g.reshape(1, d), w_main, w_ab, conv_w)


def _gdn_chunk_kernel(qkv_ref, z_ref, ab_ref, alog_ref, dtb_ref, ong_ref, o_ref, state_ref, *,
                      ts, heads, group):
    t = pl.program_id(1)
    C, Dh = GDN_CHUNK, HEAD_DIM
    W = heads * Dh
    shift = C.bit_length() - 1

    @pl.when(t == 0)
    def _():
        state_ref[...] = jnp.zeros_like(state_ref)

    ri = lax.broadcasted_iota(jnp.int32, (ts, ts), 0)
    ci = lax.broadcasted_iota(jnp.int32, (ts, ts), 1)
    same = (ri >> shift) == (ci >> shift)
    dist = jnp.where(same, ri - ci, -1)
    lower = dist >= 0
    strict = dist > 0
    eye = (dist == 0).astype(_F32)
    sel = (lax.broadcasted_iota(jnp.int32, (SUBLANES, LANES), 0)
           == lax.broadcasted_iota(jnp.int32, (SUBLANES, LANES), 1)).astype(_F32)

    ab = ab_ref[...]
    g_raw = -jnp.exp(alog_ref[...]) * jax.nn.softplus(ab + dtb_ref[...])
    beta_all = jax.nn.sigmoid(ab)
    gc = _dot_f32(lower.astype(_F32), g_raw)
    gtot = _dot_f32(same.astype(_F32), g_raw)
    gc_rows = _dot_nt_f32(sel, gc)
    eg_all = jnp.exp(gc)
    egl_all = jnp.exp(gtot)
    ekd_all = jnp.exp(gtot - gc)
    ong = ong_ref[...]

    def head(hd):
        lo = hd * Dh
        q = qkv_ref[:, lo:lo + Dh]
        k = qkv_ref[:, W + lo:W + lo + Dh]
        v = qkv_ref[:, 2 * W + lo:2 * W + lo + Dh]
        q = q * lax.rsqrt(jnp.sum(q * q, axis=-1, keepdims=True) + EPS) * (Dh ** -0.5)
        k = k * lax.rsqrt(jnp.sum(k * k, axis=-1, keepdims=True) + EPS)
        gcol = gc[:, hd:hd + 1]
        grow = gc_rows[hd:hd + 1, :]
        eg = eg_all[:, hd:hd + 1]
        beta = beta_all[:, heads + hd:heads + hd + 1]
        kb = k * beta
        vb = v * beta
        decay = jnp.where(lower, jnp.exp(jnp.where(lower, gcol - grow, 0.0)), 0.0)
        a_mat = jnp.where(strict, _dot_nt(kb, k) * decay, 0.0)
        qk = jnp.where(lower, _dot_nt(q, k) * decay, 0.0)
        yield
        t_mat = eye - a_mat
        pw = a_mat
        for _ in range(shift - 1):
            pw = _dot(pw, pw)
            yield
            t_mat = t_mat + _dot(t_mat, pw)
        yield
        uw = _dot(t_mat, jnp.concatenate([vb, kb * eg], axis=1))
        yield
        u = uw[:, :Dh]
        w = uw[:, Dh:]
        qg = q * eg
        kd = k * ekd_all[:, hd:hd + 1]
        s = state_ref[hd]
        v_new, o_state = [], []
        for n in range(ts // C):
            r = slice(n * C, (n + 1) * C)
            ws = _dot(jnp.concatenate([w[r], qg[r]], axis=0), s)
            yield
            vn = u[r] - ws[:C]
            v_new.append(vn)
            o_state.append(ws[C:])
            s = s * egl_all[n * C:n * C + 1, hd:hd + 1] + _dot_tn(kd[r], vn)
            yield
        state_ref[hd] = s
        o = jnp.concatenate(o_state, axis=0) + _dot(qk, jnp.concatenate(v_new, axis=0))
        o = _rms_rows(o, ong) * _silu(z_ref[:, lo:lo + Dh])
        o_ref[:, lo:lo + Dh] = o.astype(o_ref.dtype)

    for h0 in range(0, heads, group):
        running = [head(hd) for hd in range(h0, min(h0 + group, heads))]
        while running:
            running = [g for g in running if next(g, True) is None]


def _gdn_chunk(qkv, z, ab, a_log, dt_bias, o_norm_g, *, batch, heads, ts=256, group=8):
    m, w3 = qkv.shape
    d = z.shape[1]
    seq = m // batch
    nt = seq // ts
    row = lambda b, t: (b * nt + t, 0)
    pad = lambda v: jnp.pad(v.reshape(1, -1), ((0, 0), (0, LANES - v.shape[0])))
    return pl.pallas_call(
        functools.partial(_gdn_chunk_kernel, ts=ts, heads=heads, group=group),
        grid=(batch, nt),
        in_specs=[pl.BlockSpec((ts, w3), row), pl.BlockSpec((ts, d), row),
                  pl.BlockSpec((ts, LANES), row), _resident((1, LANES)), _resident((1, LANES)),
                  _resident((1, HEAD_DIM))],
        out_specs=pl.BlockSpec((ts, d), row),
        out_shape=jax.ShapeDtypeStruct((m, d), _BF16),
        scratch_shapes=[pltpu.VMEM((heads, HEAD_DIM, HEAD_DIM), _F32)],
        compiler_params=_params(),
        name="gdn_chunk",
    )(qkv, z, ab, pad(a_log), pad(dt_bias), o_norm_g.reshape(1, HEAD_DIM))


def _gated_deltanet(x, g, w_in, conv_w, a_log, dt_bias, o_norm_g, w_out, *, batch):
    heads = a_log.shape[0]
    qkv, z, ab = _gdn_in(x, g, w_in, conv_w, batch=batch)
    o = _gdn_chunk(qkv, z, ab, a_log, dt_bias, o_norm_g, batch=batch, heads=heads)
    return _out_proj(o, w_out, x)


def _fox_in_kernel(x_ref, g_ref, w_ref, wf_ref, bf_ref, qg_ref, kg_ref, q_ref, k_ref, v_ref, c_ref,
                   csum_ref, *, tm, heads):
    t = pl.program_id(1)
    Dh = HEAD_DIM
    W = heads * Dh

    @pl.when(t == 0)
    def _():
        csum_ref[...] = jnp.zeros_like(csum_ref)

    h = _rms_rows(x_ref[...], g_ref[...]).astype(_BF16)
    scale = Dh ** -0.5
    for hd in range(heads):
        lo = hd * Dh
        q = jnp.dot(h, w_ref[:, lo:lo + Dh], preferred_element_type=_F32)
        q_ref[:, lo:lo + Dh] = (_rms_rows(q, qg_ref[...]) * scale).astype(q_ref.dtype)
        k = jnp.dot(h, w_ref[:, W + lo:W + lo + Dh], preferred_element_type=_F32)
        k_ref[:, lo:lo + Dh] = _rms_rows(k, kg_ref[...]).astype(k_ref.dtype)
    v_ref[...] = jnp.dot(h, w_ref[:, 2 * W:3 * W], preferred_element_type=_F32).astype(v_ref.dtype)

    f = jnp.dot(h, wf_ref[...], preferred_element_type=_F32) + bf_ref[...]
    log_f = jax.nn.log_sigmoid(f)
    ri = lax.broadcasted_iota(jnp.int32, (tm, tm), 0)
    ci = lax.broadcasted_iota(jnp.int32, (tm, tm), 1)
    c = _dot_f32((ri >= ci).astype(_F32), log_f) + csum_ref[0:1, :]
    c_ref[...] = c
    csum_ref[0:1, :] = c[tm - 1:tm, :]


def _fox_in(x, g, w_in, b_f, q_norm_g, k_norm_g, *, batch, heads, tm=256):
    m, d = x.shape
    seq = m // batch
    nt = seq // tm
    W = heads * HEAD_DIM
    w_main = w_in[:, :3 * W].astype(_BF16)
    w_f = jnp.pad(w_in[:, 3 * W:], ((0, 0), (0, LANES - heads))).astype(_BF16)
    b_pad = jnp.pad(b_f.reshape(1, heads), ((0, 0), (0, LANES - heads)))
    row = lambda b, t: (b * nt + t, 0)
    return pl.pallas_call(
        functools.partial(_fox_in_kernel, tm=tm, heads=heads),
        grid=(batch, nt),
        in_specs=[pl.BlockSpec((tm, d), row), _resident((1, d)), _resident(w_main.shape),
                  _resident(w_f.shape), _resident((1, LANES)), _resident((1, HEAD_DIM)),
                  _resident((1, HEAD_DIM))],
        out_specs=[pl.BlockSpec((tm, W), row)] * 3 + [pl.BlockSpec((tm, LANES), row)],
        out_shape=[jax.ShapeDtypeStruct((m, W), _BF16)] * 3 + [jax.ShapeDtypeStruct((m, LANES), _F32)],
        scratch_shapes=[pltpu.VMEM((SUBLANES, LANES), _F32)],
        compiler_params=_params(),
        name="fox_in",
    )(x, g.reshape(1, d), w_main, w_f, b_pad, q_norm_g.reshape(1, HEAD_DIM),
      k_norm_g.reshape(1, HEAD_DIM))


def _fox_attn_kernel(q_ref, k_ref, v_ref, ccol_ref, crow_ref, o_ref, *, tq, heads):
    qi = pl.program_id(1)
    Dh = HEAD_DIM
    ri = lax.broadcasted_iota(jnp.int32, (tq, tq), 0)
    ci = lax.broadcasted_iota(jnp.int32, (tq, tq), 1)
    causal = ci <= ri

    for hd in range(heads):
        lo = hd * Dh
        q = q_ref[:, lo:lo + Dh]
        cq = ccol_ref[:, hd:hd + 1]

        def logits(j):
            k0 = pl.multiple_of(j * tq, tq)
            s = _dot_nt(q, k_ref[pl.ds(k0, tq), lo:lo + Dh])
            return s + cq - crow_ref[hd, pl.ds(j, 1), :], k0

        def update(s, k0, m_run, l_run, acc):
            m_new = jnp.maximum(m_run, jnp.max(s, axis=-1, keepdims=True))
            alpha = jnp.exp(m_run - m_new)
            p = jnp.exp(s - m_new)
            l_new = alpha * l_run + jnp.sum(p, axis=-1, keepdims=True)
            acc = alpha * acc + _dot(p, v_ref[pl.ds(k0, tq), lo:lo + Dh])
            return m_new, l_new, acc

        def body(j, carry):
            s, k0 = logits(j)
            return update(s, k0, *carry)

        init = (jnp.full((tq, 1), -jnp.inf, _F32), jnp.zeros((tq, 1), _F32),
                jnp.zeros((tq, Dh), _F32))
        carry = lax.fori_loop(0, qi, body, init)
        s, k0 = logits(qi)
        s = jnp.where(causal, s, -jnp.inf)
        _, l_fin, acc = update(s, k0, *carry)
        o_ref[:, lo:lo + Dh] = (acc / l_fin).astype(o_ref.dtype)


def _fox_attn(q, k, v, c, *, batch, heads, tq=256):
    m, W = q.shape
    seq = m // batch
    nq = seq // tq
    c_rows = c[:, :heads].reshape(batch, seq, heads).transpose(0, 2, 1).reshape(batch, heads, nq, tq)
    row = lambda b, t: (b * nq + t, 0)
    return pl.pallas_call(
        functools.partial(_fox_attn_kernel, tq=tq, heads=heads),
        grid=(batch, nq),
        in_specs=[pl.BlockSpec((tq, W), row), pl.BlockSpec((seq, W), lambda b, t: (b, 0)),
                  pl.BlockSpec((seq, W), lambda b, t: (b, 0)), pl.BlockSpec((tq, LANES), row),
                  pl.BlockSpec((None, heads, nq, tq), lambda b, t: (b, 0, 0, 0))],
        out_specs=pl.BlockSpec((tq, W), row),
        out_shape=jax.ShapeDtypeStruct((m, W), _BF16),
        compiler_params=_params(),
        name="fox_attn",
    )(q, k, v, c, c_rows)


def _forgetting_attention(x, g, w_in, b_f, q_norm_g, k_norm_g, w_out, *, batch):
    heads = b_f.shape[0]
    q, k, v, c = _fox_in(x, g, w_in, b_f, q_norm_g, k_norm_g, batch=batch, heads=heads)
    o = _fox_attn(q, k, v, c, batch=batch, heads=heads)
    return _out_proj(o, w_out, x)


def kernel(x, mix_norm_g, ffn_norm_g, conv_w_in, conv_b_in, conv_w_dw, conv_b_dw, conv_ln_g, conv_ln_b,
           conv_w_out, gdn_w_in, gdn_conv_w, gdn_a_log, gdn_dt_bias, gdn_o_norm_g, gdn_w_out, fox_w_in,
           fox_b_f, fox_q_norm_g, fox_k_norm_g, fox_w_out, ffn_w_up, ffn_w_dw, ffn_w_down):
    batch, seq, d = x.shape
    depth = mix_norm_g.shape[0]
    xs = x.reshape(batch * seq, d)
    ia = ib = ic = 0
    for layer in range(depth):
        kind = layer % N_MIXERS
        g = mix_norm_g[layer]
        if kind == 0:
            xs = _conformer(xs, g, conv_w_in[ia], conv_b_in[ia], conv_w_dw[ia], conv_b_dw[ia],
                            conv_ln_g[ia], conv_ln_b[ia], conv_w_out[ia], batch=batch)
            ia += 1
        elif kind == 1:
            xs = _gated_deltanet(xs, g, gdn_w_in[ib], gdn_conv_w[ib], gdn_a_log[ib], gdn_dt_bias[ib],
                                 gdn_o_norm_g[ib], gdn_w_out[ib], batch=batch)
            ib += 1
        else:
            xs = _forgetting_attention(xs, g, fox_w_in[ic], fox_b_f[ic], fox_q_norm_g[ic],
                                       fox_k_norm_g[ic], fox_w_out[ic], batch=batch)
            ic += 1
        xs = _conv_ffn(xs, ffn_norm_g[layer], ffn_w_up[layer], ffn_w_dw[layer], ffn_w_down[layer],
                       batch=batch)
    return xs.reshape(batch, seq, d)
```

```python
import functools

import jax
import jax.numpy as jnp
from jax import lax
from jax.experimental import pallas as pl
from jax.experimental.pallas import tpu as pltpu

EPS = 1e-6
N_MIXERS = 3
HEAD_DIM = 128
GDN_CHUNK = 64
LANES = 128
SUBLANES = 8
VMEM_LIMIT = 56 * 1024 * 1024

_BF16 = jnp.bfloat16
_F32 = jnp.float32


def _dot(a, b):
    return jnp.dot(a.astype(_BF16), b.astype(_BF16), preferred_element_type=_F32)


def _dot_nt(a, b):
    return lax.dot_general(a.astype(_BF16), b.astype(_BF16), (((1,), (1,)), ((), ())),
                           preferred_element_type=_F32)


def _dot_tn(a, b):
    return lax.dot_general(a.astype(_BF16), b.astype(_BF16), (((0,), (0,)), ((), ())),
                           preferred_element_type=_F32)


def _dot_f32(a, b):
    return jnp.dot(a, b, preferred_element_type=_F32, precision=lax.Precision.HIGHEST)


def _dot_nt_f32(a, b):
    return lax.dot_general(a, b, (((1,), (1,)), ((), ())), preferred_element_type=_F32,
                           precision=lax.Precision.HIGHEST)


def _rms_rows(x, g):
    return x * lax.rsqrt(jnp.mean(x * x, axis=-1, keepdims=True) + EPS) * g


def _silu(x):
    return x * jax.nn.sigmoid(x)


def _round_robin(gens):
    gens = list(gens)
    while gens:
        gens = [g for g in gens if next(g, True) is None]


def _tap_rows(w_ref, col, width):
    return [jnp.broadcast_to(w_ref[k:k + 1, col:col + width], (SUBLANES, width))
            for k in range(w_ref.shape[0])]


def _short_conv_tile(buf, r0, rc, w_rows):
    taps = len(w_rows)
    n = rc // SUBLANES
    groups = [buf[r0 + SUBLANES * i:r0 + SUBLANES * (i + 1), :] for i in range(n + 1)]
    sub = lax.broadcasted_iota(jnp.int32, groups[0].shape, 0)
    ys = None
    for k in range(taps):
        shift = taps - 1 - k
        if shift:
            rolled = [pltpu.roll(p, shift, axis=0) for p in groups]
            src = [jnp.where(sub < shift, rolled[i], rolled[i + 1]) for i in range(n)]
        else:
            src = groups[1:]
        terms = [p * w_rows[k] for p in src]
        ys = terms if ys is None else [a + b for a, b in zip(ys, terms)]
    return jnp.concatenate(ys, axis=0)


def _resident(shape):
    nd = len(shape)
    return pl.BlockSpec(shape, lambda *_: (0,) * nd, pipeline_mode=pl.Buffered(1))


def _params(n_axes=2):
    sem = ("parallel",) + ("arbitrary",) * (n_axes - 1)
    return pltpu.CompilerParams(dimension_semantics=sem, vmem_limit_bytes=VMEM_LIMIT)


def _ffn_kernel(x_ref, g_ref, wup_ref, wdw_ref, wdown_ref, o_ref, carry_ref, ubuf_ref, act_ref, *,
                tm, d_ff, cw, rc, dg):
    t = pl.program_id(1)

    @pl.when(t == 0)
    def _():
        carry_ref[...] = jnp.zeros_like(carry_ref)

    x = x_ref[...]
    h = _rms_rows(x, g_ref[...]).astype(_BF16)
    acc = x
    taps = wdw_ref.shape[0]
    n_chunks = d_ff // cw
    n_buf = ubuf_ref.shape[0]

    def up(c):
        for half in range(2):
            col = half * d_ff + c * cw
            buf = ubuf_ref.at[(2 * c + half) % n_buf]
            buf[0:SUBLANES, :] = carry_ref[:, col:col + cw]
            buf[SUBLANES:SUBLANES + tm, :] = jnp.dot(h, wup_ref[:, col:col + cw],
                                                     preferred_element_type=_F32)
            carry_ref[:, col:col + cw] = buf[tm:tm + SUBLANES, :]

    def tap_rows(c, half):
        col = half * d_ff + c * cw
        return _tap_rows(wdw_ref, col, cw)

    def conv(c, half, r0, w_rows):
        return _short_conv_tile(ubuf_ref.at[(2 * c + half) % n_buf], r0, rc, w_rows)

    up(0)
    for c in range(n_chunks):
        if c + 1 < n_chunks:
            up(c + 1)
        w_gate, w_lin = tap_rows(c, 0), tap_rows(c, 1)
        for r0 in range(0, tm, rc):
            act_ref[r0:r0 + rc, c * cw:(c + 1) * cw] = (
                _silu(conv(c, 0, r0, w_gate)) * conv(c, 1, r0, w_lin)).astype(_BF16)
        if (c + 1) % dg == 0 or c + 1 == n_chunks:
            k0 = (c // dg) * dg * cw
            acc = acc + jnp.dot(act_ref[:, k0:(c + 1) * cw], wdown_ref[k0:(c + 1) * cw, :],
                                preferred_element_type=_F32)
    o_ref[...] = acc


def _conv_ffn(x, g, w_up, w_dw, w_down, *, batch, tm=512, cw=256, rc=64, dg=4):
    m, d = x.shape
    seq = m // batch
    d_ff = w_down.shape[0]
    nt = seq // tm
    row = lambda b, t: (b * nt + t, 0)
    return pl.pallas_call(
        functools.partial(_ffn_kernel, tm=tm, d_ff=d_ff, cw=cw, rc=rc, dg=dg),
        grid=(batch, nt),
        in_specs=[pl.BlockSpec((tm, d), row), _resident((1, d)), _resident(w_up.shape),
                  _resident(w_dw.shape), _resident(w_down.shape)],
        out_specs=pl.BlockSpec((tm, d), row),
        out_shape=jax.ShapeDtypeStruct((m, d), _F32),
        scratch_shapes=[pltpu.VMEM((SUBLANES, 2 * d_ff), _F32),
                        pltpu.VMEM((4, tm + SUBLANES, cw), _F32),
                        pltpu.VMEM((tm, d_ff), _BF16)],
        compiler_params=_params(),
        name="conv_ffn",
    )(x, g.reshape(1, d), w_up.astype(_BF16), w_dw, w_down.astype(_BF16))


def _dwconv_tiles(ubuf_ref, w_ref, b_ref, y_ref, zbuf_ref, *, tm, halo, rc):
    taps, d = w_ref.shape
    base = halo - (taps - 1)
    slabs = -(-(base + taps) // SUBLANES)

    def body(i, carry):
        r0 = pl.multiple_of(i * rc, rc)
        for c0 in range(0, d, LANES):
            cols = slice(c0, c0 + LANES)
            acc = jnp.broadcast_to(b_ref[:, cols], (rc, LANES))
            for r in range(SUBLANES):
                rows = rc if r == 0 else rc + SUBLANES
                z = None
                for a in range(slabs):
                    k = SUBLANES * a + r - base
                    if 0 <= k < taps:
                        term = ubuf_ref[pl.ds(r0 + SUBLANES * a, rows), cols] * w_ref[k:k + 1, cols]
                        z = term if z is None else z + term
                if z is None:
                    continue
                if r == 0:
                    acc = acc + z
                else:
                    zbuf_ref[r, :, cols] = z
                    acc = acc + zbuf_ref[r, pl.ds(r, rc), cols]
            y_ref[pl.ds(r0, rc), cols] = acc
        return carry

    lax.fori_loop(0, tm // rc, body, 0)


def _conformer_kernel(x_ref, g_ref, win_ref, bin_ref, wdw_ref, bdw_ref, lng_ref, lnb_ref, wout_ref,
                      o_ref, ubuf_ref, y_ref, zbuf_ref, *, tm, halo, rc):
    t = pl.program_id(1)
    d = x_ref.shape[1]

    @pl.when(t == 0)
    def _():
        ubuf_ref[0:halo, :] = jnp.zeros((halo, d), _F32)

    x = x_ref[...]
    h = _rms_rows(x, g_ref[...]).astype(_BF16)
    val = jnp.dot(h, win_ref[:, 0:d], preferred_element_type=_F32) + bin_ref[:, 0:d]
    gate = jnp.dot(h, win_ref[:, d:2 * d], preferred_element_type=_F32) + bin_ref[:, d:2 * d]
    ubuf_ref[halo:halo + tm, :] = val * jax.nn.sigmoid(gate)

    _dwconv_tiles(ubuf_ref, wdw_ref, bdw_ref, y_ref, zbuf_ref, tm=tm, halo=halo, rc=rc)
    ubuf_ref[0:halo, :] = ubuf_ref[tm:tm + halo, :]

    y = y_ref[...]
    yc = y - jnp.mean(y, axis=-1, keepdims=True)
    var = jnp.mean(yc * yc, axis=-1, keepdims=True)
    z = _silu(yc * lax.rsqrt(var + EPS) * lng_ref[...] + lnb_ref[...])
    o_ref[...] = x + jnp.dot(z.astype(_BF16), wout_ref[...], preferred_element_type=_F32)


def _conformer(x, g, w_in, b_in, w_dw, b_dw, ln_g, ln_b, w_out, *, batch, tm=256, rc=64):
    m, d = x.shape
    seq = m // batch
    nt = seq // tm
    taps = w_dw.shape[0]
    halo = -(-(taps - 1) // SUBLANES) * SUBLANES
    row = lambda b, t: (b * nt + t, 0)
    return pl.pallas_call(
        functools.partial(_conformer_kernel, tm=tm, halo=halo, rc=rc),
        grid=(batch, nt),
        in_specs=[pl.BlockSpec((tm, d), row), _resident((1, d)), _resident(w_in.shape),
                  _resident((1, 2 * d)), _resident(w_dw.shape), _resident((1, d)),
                  _resident((1, d)), _resident((1, d)), _resident(w_out.shape)],
        out_specs=pl.BlockSpec((tm, d), row),
        out_shape=jax.ShapeDtypeStruct((m, d), _F32),
        scratch_shapes=[pltpu.VMEM((tm + halo, d), _F32), pltpu.VMEM((tm, d), _F32),
                        pltpu.VMEM((SUBLANES, rc + SUBLANES, d), _F32)],
        compiler_params=_params(),
        name="conformer",
    )(x, g.reshape(1, d), w_in.astype(_BF16), b_in.reshape(1, 2 * d), w_dw, b_dw.reshape(1, d),
      ln_g.reshape(1, d), ln_b.reshape(1, d), w_out.astype(_BF16))


def _out_proj_kernel(a_ref, w_ref, res_ref, o_ref):
    o_ref[...] = res_ref[...] + jnp.dot(a_ref[...], w_ref[...], preferred_element_type=_F32)


def _out_proj(a, w, res, *, tm=1024):
    m, k = a.shape
    d = w.shape[1]
    return pl.pallas_call(
        _out_proj_kernel,
        grid=(m // tm,),
        in_specs=[pl.BlockSpec((tm, k), lambda i: (i, 0)), _resident(w.shape),
                  pl.BlockSpec((tm, d), lambda i: (i, 0))],
        out_specs=pl.BlockSpec((tm, d), lambda i: (i, 0)),
        out_shape=jax.ShapeDtypeStruct((m, d), _F32),
        compiler_params=_params(1),
        name="out_proj",
    )(a, w.astype(_BF16), res)


def _gdn_in_kernel(x_ref, g_ref, w_ref, wab_ref, cw_ref, qkv_ref, z_ref, ab_ref, carry_ref, ubuf_ref,
                   *, tm, cwid, n_conv, rc):
    t = pl.program_id(1)

    @pl.when(t == 0)
    def _():
        carry_ref[...] = jnp.zeros_like(carry_ref)

    h = _rms_rows(x_ref[...], g_ref[...]).astype(_BF16)
    n_chunks = n_conv // cwid
    n_buf = ubuf_ref.shape[0]

    def up(c):
        col = c * cwid
        buf = ubuf_ref.at[c % n_buf]
        buf[0:SUBLANES, :] = carry_ref[:, col:col + cwid]
        buf[SUBLANES:SUBLANES + tm, :] = jnp.dot(h, w_ref[:, col:col + cwid], preferred_element_type=_F32)
        carry_ref[:, col:col + cwid] = buf[tm:tm + SUBLANES, :]

    up(0)
    for c in range(n_chunks):
        if c + 1 < n_chunks:
            up(c + 1)
        else:
            z_ref[...] = jnp.dot(h, w_ref[:, n_conv:], preferred_element_type=_F32)
            ab_ref[...] = jnp.dot(h, wab_ref[...], preferred_element_type=_F32)
        w_rows = _tap_rows(cw_ref, c * cwid, cwid)
        for r0 in range(0, tm, rc):
            qkv_ref[r0:r0 + rc, c * cwid:(c + 1) * cwid] = _silu(
                _short_conv_tile(ubuf_ref.at[c % n_buf], r0, rc, w_rows))


def _gdn_in(x, g, w_in, conv_w, *, batch, tm=512, cwid=256, rc=64):
    m, d = x.shape
    seq = m // batch
    nt = seq // tm
    n_conv = conv_w.shape[1]
    wz = w_in.shape[1] - n_conv
    w_main = w_in[:, :n_conv + d].astype(_BF16)
    n_ab = w_in.shape[1] - (n_conv + d)
    w_ab = jnp.pad(w_in[:, n_conv + d:], ((0, 0), (0, LANES - n_ab))).astype(_BF16)
    del wz
    row = lambda b, t: (b * nt + t, 0)
    return pl.pallas_call(
        functools.partial(_gdn_in_kernel, tm=tm, cwid=cwid, n_conv=n_conv, rc=rc),
        grid=(batch, nt),
        in_specs=[pl.BlockSpec((tm, d), row), _resident((1, d)), _resident(w_main.shape),
                  _resident(w_ab.shape), _resident(conv_w.shape)],
        out_specs=[pl.BlockSpec((tm, n_conv), row), pl.BlockSpec((tm, d), row),
                   pl.BlockSpec((tm, LANES), row)],
        out_shape=[jax.ShapeDtypeStruct((m, n_conv), _F32), jax.ShapeDtypeStruct((m, d), _F32),
                   jax.ShapeDtypeStruct((m, LANES), _F32)],
        scratch_shapes=[pltpu.VMEM((SUBLANES, n_conv), _F32),
                        pltpu.VMEM((3, tm + SUBLANES, cwid), _F32)],
        compiler_params=_params(),
        name="gdn_in",
    )(x, g.reshape(1, d), w_main, w_ab, conv_w)


def _gdn_chunk_kernel(qkv_ref, z_ref, ab_ref, alog_ref, dtb_ref, ong_ref, o_ref, state_ref, *,
                      ts, heads, group):
    t = pl.program_id(1)
    C, Dh = GDN_CHUNK, HEAD_DIM
    W = heads * Dh
    shift = C.bit_length() - 1

    @pl.when(t == 0)
    def _():
        state_ref[...] = jnp.zeros_like(state_ref)

    ri = lax.broadcasted_iota(jnp.int32, (ts, ts), 0)
    ci = lax.broadcasted_iota(jnp.int32, (ts, ts), 1)
    same = (ri >> shift) == (ci >> shift)
    dist = jnp.where(same, ri - ci, -1)
    lower = dist >= 0
    strict = dist > 0
    eye = (dist == 0).astype(_F32)
    sel = (lax.broadcasted_iota(jnp.int32, (SUBLANES, LANES), 0)
           == lax.broadcasted_iota(jnp.int32, (SUBLANES, LANES), 1)).astype(_F32)

    ab = ab_ref[...]
    g_raw = -jnp.exp(alog_ref[...]) * jax.nn.softplus(ab + dtb_ref[...])
    beta_all = jax.nn.sigmoid(ab)
    gc = _dot_f32(lower.astype(_F32), g_raw)
    gtot = _dot_f32(same.astype(_F32), g_raw)
    gc_rows = _dot_nt_f32(sel, gc)
    eg_all = jnp.exp(gc)
    egl_all = jnp.exp(gtot)
    ekd_all = jnp.exp(gtot - gc)
    ong = ong_ref[...]

    def head(hd):
        lo = hd * Dh
        q = qkv_ref[:, lo:lo + Dh]
        k = qkv_ref[:, W + lo:W + lo + Dh]
        v = qkv_ref[:, 2 * W + lo:2 * W + lo + Dh]
        q = q * lax.rsqrt(jnp.sum(q * q, axis=-1, keepdims=True) + EPS) * (Dh ** -0.5)
        k = k * lax.rsqrt(jnp.sum(k * k, axis=-1, keepdims=True) + EPS)
        gcol = gc[:, hd:hd + 1]
        grow = gc_rows[hd:hd + 1, :]
        eg = eg_all[:, hd:hd + 1]
        beta = beta_all[:, heads + hd:heads + hd + 1]
        kb = k * beta
        vb = v * beta
        decay = jnp.where(lower, jnp.exp(jnp.where(lower, gcol - grow, 0.0)), 0.0)
        a_mat = jnp.where(strict, _dot_nt(kb, k) * decay, 0.0)
        qk = jnp.where(lower, _dot_nt(q, k) * decay, 0.0)
        yield
        t_mat = eye - a_mat
        pw = a_mat
        for _ in range(shift - 1):
            pw = _dot(pw, pw)
            yield
            t_mat = t_mat + _dot(t_mat, pw)
        yield
        uw = _dot(t_mat, jnp.concatenate([vb, kb * eg], axis=1))
        yield
        u = uw[:, :Dh]
        w = uw[:, Dh:]
        qg = q * eg
        kd = k * ekd_all[:, hd:hd + 1]
        s = state_ref[hd]
        v_new, o_state = [], []
        for n in range(ts // C):
            r = slice(n * C, (n + 1) * C)
            ws = _dot(jnp.concatenate([w[r], qg[r]], axis=0), s)
            yield
            vn = u[r] - ws[:C]
            v_new.append(vn)
            o_state.append(ws[C:])
            s = s * egl_all[n * C:n * C + 1, hd:hd + 1] + _dot_tn(kd[r], vn)
            yield
        state_ref[hd] = s
        o = jnp.concatenate(o_state, axis=0) + _dot(qk, jnp.concatenate(v_new, axis=0))
        o = _rms_rows(o, ong) * _silu(z_ref[:, lo:lo + Dh])
        o_ref[:, lo:lo + Dh] = o.astype(o_ref.dtype)

    for h0 in range(0, heads, group):
        _round_robin(head(hd) for hd in range(h0, min(h0 + group, heads)))


def _gdn_chunk(qkv, z, ab, a_log, dt_bias, o_norm_g, *, batch, heads, ts=256, group=8):
    m, w3 = qkv.shape
    d = z.shape[1]
    seq = m // batch
    nt = seq // ts
    row = lambda b, t: (b * nt + t, 0)
    pad = lambda v: jnp.pad(v.reshape(1, -1), ((0, 0), (0, LANES - v.shape[0])))
    return pl.pallas_call(
        functools.partial(_gdn_chunk_kernel, ts=ts, heads=heads, group=group),
        grid=(batch, nt),
        in_specs=[pl.BlockSpec((ts, w3), row), pl.BlockSpec((ts, d), row),
                  pl.BlockSpec((ts, LANES), row), _resident((1, LANES)), _resident((1, LANES)),
                  _resident((1, HEAD_DIM))],
        out_specs=pl.BlockSpec((ts, d), row),
        out_shape=jax.ShapeDtypeStruct((m, d), _BF16),
        scratch_shapes=[pltpu.VMEM((heads, HEAD_DIM, HEAD_DIM), _F32)],
        compiler_params=_params(),
        name="gdn_chunk",
    )(qkv, z, ab, pad(a_log), pad(dt_bias), o_norm_g.reshape(1, HEAD_DIM))


def _gated_deltanet(x, g, w_in, conv_w, a_log, dt_bias, o_norm_g, w_out, *, batch):
    heads = a_log.shape[0]
    qkv, z, ab = _gdn_in(x, g, w_in, conv_w, batch=batch)
    o = _gdn_chunk(qkv, z, ab, a_log, dt_bias, o_norm_g, batch=batch, heads=heads)
    return _out_proj(o, w_out, x)


def _fox_in_kernel(x_ref, g_ref, w_ref, wf_ref, bf_ref, qg_ref, kg_ref, q_ref, k_ref, v_ref, c_ref,
                   csum_ref, *, tm, heads):
    t = pl.program_id(1)
    Dh = HEAD_DIM
    W = heads * Dh

    @pl.when(t == 0)
    def _():
        csum_ref[...] = jnp.zeros_like(csum_ref)

    h = _rms_rows(x_ref[...], g_ref[...]).astype(_BF16)
    scale = Dh ** -0.5
    for hd in range(heads):
        lo = hd * Dh
        q = jnp.dot(h, w_ref[:, lo:lo + Dh], preferred_element_type=_F32)
        q_ref[:, lo:lo + Dh] = (_rms_rows(q, qg_ref[...]) * scale).astype(q_ref.dtype)
        k = jnp.dot(h, w_ref[:, W + lo:W + lo + Dh], preferred_element_type=_F32)
        k_ref[:, lo:lo + Dh] = _rms_rows(k, kg_ref[...]).astype(k_ref.dtype)
    v_ref[...] = jnp.dot(h, w_ref[:, 2 * W:3 * W], preferred_element_type=_F32).astype(v_ref.dtype)

    f = jnp.dot(h, wf_ref[...], preferred_element_type=_F32) + bf_ref[...]
    log_f = jax.nn.log_sigmoid(f)
    ri = lax.broadcasted_iota(jnp.int32, (tm, tm), 0)
    ci = lax.broadcasted_iota(jnp.int32, (tm, tm), 1)
    c = _dot_f32((ri >= ci).astype(_F32), log_f) + csum_ref[0:1, :]
    c_ref[...] = c
    csum_ref[0:1, :] = c[tm - 1:tm, :]


def _fox_in(x, g, w_in, b_f, q_norm_g, k_norm_g, *, batch, heads, tm=256):
    m, d = x.shape
    seq = m // batch
    nt = seq // tm
    W = heads * HEAD_DIM
    w_main = w_in[:, :3 * W].astype(_BF16)
    w_f = jnp.pad(w_in[:, 3 * W:], ((0, 0), (0, LANES - heads))).astype(_BF16)
    b_pad = jnp.pad(b_f.reshape(1, heads), ((0, 0), (0, LANES - heads)))
    row = lambda b, t: (b * nt + t, 0)
    return pl.pallas_call(
        functools.partial(_fox_in_kernel, tm=tm, heads=heads),
        grid=(batch, nt),
        in_specs=[pl.BlockSpec((tm, d), row), _resident((1, d)), _resident(w_main.shape),
                  _resident(w_f.shape), _resident((1, LANES)), _resident((1, HEAD_DIM)),
                  _resident((1, HEAD_DIM))],
        out_specs=[pl.BlockSpec((tm, W), row)] * 3 + [pl.BlockSpec((tm, LANES), row)],
        out_shape=[jax.ShapeDtypeStruct((m, W), _BF16)] * 3 + [jax.ShapeDtypeStruct((m, LANES), _F32)],
        scratch_shapes=[pltpu.VMEM((SUBLANES, LANES), _F32)],
        compiler_params=_params(),
        name="fox_in",
    )(x, g.reshape(1, d), w_main, w_f, b_pad, q_norm_g.reshape(1, HEAD_DIM),
      k_norm_g.reshape(1, HEAD_DIM))


def _fox_attn_kernel(q_ref, k_ref, v_ref, ccol_ref, crow_ref, o_ref, m_ref, l_ref, acc_ref, cq_ref, *,
                     tq, heads):
    qi = pl.program_id(1)
    Dh = HEAD_DIM
    ri = lax.broadcasted_iota(jnp.int32, (tq, Dh), 0)
    ci = lax.broadcasted_iota(jnp.int32, (tq, Dh), 1)
    halves = tq // Dh
    causal = [ci + hf * Dh <= ri for hf in range(halves)]
    ones = jnp.ones((tq, Dh), _BF16)

    m_ref[...] = jnp.full(m_ref.shape, -jnp.inf, _F32)
    l_ref[...] = jnp.zeros(l_ref.shape, _F32)
    acc_ref[...] = jnp.zeros(acc_ref.shape, _F32)
    for hd in range(heads):
        cq_ref[hd] = jnp.broadcast_to(ccol_ref[:, hd:hd + 1], (tq, Dh))

    def head_step(hd, j, k0, masked):
        lo = hd * Dh
        s = _dot_nt(q_ref[:, lo:lo + Dh], k_ref[pl.ds(k0, tq), lo:lo + Dh])
        yield
        crow = crow_ref[hd, pl.ds(j, 1), :]
        parts = []
        for hf in range(halves):
            sh = s[:, hf * Dh:(hf + 1) * Dh] + cq_ref[hd] - crow[:, hf * Dh:(hf + 1) * Dh]
            parts.append(jnp.where(causal[hf], sh, -jnp.inf) if masked else sh)
        m_old = m_ref[hd]
        m_blk = functools.reduce(jnp.maximum, parts)
        m_new = jnp.maximum(m_old, jnp.broadcast_to(jnp.max(m_blk, axis=-1, keepdims=True), (tq, Dh)))
        yield
        alpha = jnp.exp(m_old - m_new)
        p = jnp.concatenate([jnp.exp(sh - m_new).astype(_BF16) for sh in parts], axis=1)
        yield
        pv = jnp.dot(p, jnp.concatenate([v_ref[pl.ds(k0, tq), lo:lo + Dh], ones], axis=1),
                     preferred_element_type=_F32)
        yield
        m_ref[hd] = m_new
        l_ref[hd] = alpha * l_ref[hd] + pv[:, Dh:]
        acc_ref[:, lo:lo + Dh] = alpha * acc_ref[:, lo:lo + Dh] + pv[:, :Dh]

    def step(j, masked):
        k0 = pl.multiple_of(j * tq, tq)
        _round_robin([head_step(hd, j, k0, masked) for hd in range(heads)])

    def body(j, carry):
        step(j, False)
        return carry

    lax.fori_loop(0, qi, body, 0)
    step(qi, True)
    for hd in range(heads):
        lo = hd * Dh
        o_ref[:, lo:lo + Dh] = (acc_ref[:, lo:lo + Dh] / l_ref[hd]).astype(o_ref.dtype)


def _fox_attn(q, k, v, c, *, batch, heads, tq=256):
    m, W = q.shape
    seq = m // batch
    nq = seq // tq
    c_rows = c[:, :heads].reshape(batch, seq, heads).transpose(0, 2, 1).reshape(batch, heads, nq, tq)
    row = lambda b, t: (b * nq + t, 0)
    return pl.pallas_call(
        functools.partial(_fox_attn_kernel, tq=tq, heads=heads),
        grid=(batch, nq),
        in_specs=[pl.BlockSpec((tq, W), row), pl.BlockSpec((seq, W), lambda b, t: (b, 0)),
                  pl.BlockSpec((seq, W), lambda b, t: (b, 0)), pl.BlockSpec((tq, LANES), row),
                  pl.BlockSpec((None, heads, nq, tq), lambda b, t: (b, 0, 0, 0))],
        out_specs=pl.BlockSpec((tq, W), row),
        out_shape=jax.ShapeDtypeStruct((m, W), _BF16),
        scratch_shapes=[pltpu.VMEM((heads, tq, HEAD_DIM), _F32), pltpu.VMEM((heads, tq, HEAD_DIM), _F32),
                        pltpu.VMEM((tq, W), _F32), pltpu.VMEM((heads, tq, HEAD_DIM), _F32)],
        compiler_params=_params(),
        name="fox_attn",
    )(q, k, v, c, c_rows)


def _forgetting_attention(x, g, w_in, b_f, q_norm_g, k_norm_g, w_out, *, batch):
    heads = b_f.shape[0]
    q, k, v, c = _fox_in(x, g, w_in, b_f, q_norm_g, k_norm_g, batch=batch, heads=heads)
    o = _fox_attn(q, k, v, c, batch=batch, heads=heads)
    return _out_proj(o, w_out, x)


def kernel(x, mix_norm_g, ffn_norm_g, conv_w_in, conv_b_in, conv_w_dw, conv_b_dw, conv_ln_g, conv_ln_b,
           conv_w_out, gdn_w_in, gdn_conv_w, gdn_a_log, gdn_dt_bias, gdn_o_norm_g, gdn_w_out, fox_w_in,
           fox_b_f, fox_q_norm_g, fox_k_norm_g, fox_w_out, ffn_w_up, ffn_w_dw, ffn_w_down):
    batch, seq, d = x.shape
    depth = mix_norm_g.shape[0]
    xs = x.reshape(batch * seq, d)
    ia = ib = ic = 0
    for layer in range(depth):
        kind = layer % N_MIXERS
        g = mix_norm_g[layer]
        if kind == 0:
            xs = _conformer(xs, g, conv_w_in[ia], conv_b_in[ia], conv_w_dw[ia], conv_b_dw[ia],
                            conv_ln_g[ia], conv_ln_b[ia], conv_w_out[ia], batch=batch)
            ia += 1
        elif kind == 1:
            xs = _gated_deltanet(xs, g, gdn_w_in[ib], gdn_conv_w[ib], gdn_a_log[ib], gdn_dt_bias[ib],
                                 gdn_o_norm_g[ib], gdn_w_out[ib], batch=batch)
            ib += 1
        else:
            xs = _forgetting_attention(xs, g, fox_w_in[ic], fox_b_f[ic], fox_q_norm_g[ic],
                                       fox_k_norm_g[ic], fox_w_out[ic], batch=batch)
            ic += 1
        xs = _conv_ffn(xs, ffn_norm_g[layer], ffn_w_up[layer], ffn_w_dw[layer], ffn_w_down[layer],
                       batch=batch)
    return xs.reshape(batch, seq, d)
```

```python
import functools

import jax
import jax.numpy as jnp
from jax import lax
from jax.experimental import pallas as pl
from jax.experimental.pallas import tpu as pltpu

EPS = 1e-6
N_MIXERS = 3
HEAD_DIM = 128
GDN_CHUNK = 64
LANES = 128
SUBLANES = 8
VMEM_LIMIT = 56 * 1024 * 1024

_BF16 = jnp.bfloat16
_F32 = jnp.float32


def _dot(a, b):
    return jnp.dot(a.astype(_BF16), b.astype(_BF16), preferred_element_type=_F32)


def _dot_nt(a, b):
    return lax.dot_general(a.astype(_BF16), b.astype(_BF16), (((1,), (1,)), ((), ())),
                           preferred_element_type=_F32)


def _dot_tn(a, b):
    return lax.dot_general(a.astype(_BF16), b.astype(_BF16), (((0,), (0,)), ((), ())),
                           preferred_element_type=_F32)


def _dot_sel(sel, x, *, transpose_x=False):
    dims = (((1,), (1 if transpose_x else 0,)), ((), ()))
    hi = x.astype(_BF16)
    rem = x - hi.astype(_F32)
    mid = rem.astype(_BF16)
    lo = (rem - mid.astype(_F32)).astype(_BF16)
    s = sel.astype(_BF16)
    return sum(lax.dot_general(s, p, dims, preferred_element_type=_F32) for p in (hi, mid, lo))


def _rms_rows(x, g):
    return x * lax.rsqrt(jnp.mean(x * x, axis=-1, keepdims=True) + EPS) * g


def _silu(x):
    return x * jax.nn.sigmoid(x)


def _round_robin(gens):
    gens = list(gens)
    while gens:
        gens = [g for g in gens if next(g, True) is None]


def _tap_rows(w_ref, col, width):
    return [jnp.broadcast_to(w_ref[k:k + 1, col:col + width], (SUBLANES, width))
            for k in range(w_ref.shape[0])]


def _short_conv_tile(buf, r0, rc, w_rows):
    taps = len(w_rows)
    n = rc // SUBLANES
    groups = [buf[r0 + SUBLANES * i:r0 + SUBLANES * (i + 1), :] for i in range(n + 1)]
    sub = lax.broadcasted_iota(jnp.int32, groups[0].shape, 0)
    ys = None
    for k in range(taps):
        shift = taps - 1 - k
        if shift:
            rolled = [pltpu.roll(p, shift, axis=0) for p in groups]
            src = [jnp.where(sub < shift, rolled[i], rolled[i + 1]) for i in range(n)]
        else:
            src = groups[1:]
        terms = [p * w_rows[k] for p in src]
        ys = terms if ys is None else [a + b for a, b in zip(ys, terms)]
    return jnp.concatenate(ys, axis=0)


def _resident(shape):
    nd = len(shape)
    return pl.BlockSpec(shape, lambda *_: (0,) * nd, pipeline_mode=pl.Buffered(1))


def _resident_layer(stack, layer, width=None):
    _, r, c = stack.shape
    return pl.BlockSpec((None, r, width or c), lambda *_: (layer, 0, 0), pipeline_mode=pl.Buffered(1))


def _params(n_axes=2):
    sem = ("parallel",) + ("arbitrary",) * (n_axes - 1)
    return pltpu.CompilerParams(dimension_semantics=sem, vmem_limit_bytes=VMEM_LIMIT)


def _ffn_kernel(x_ref, g_ref, wup_ref, wdw_ref, wdown_ref, o_ref, carry_ref, ubuf_ref, act_ref, *,
                tm, d_ff, cw, rc, dg):
    t = pl.program_id(1)

    @pl.when(t == 0)
    def _():
        carry_ref[...] = jnp.zeros_like(carry_ref)

    x = x_ref[...]
    h = _rms_rows(x, g_ref[...]).astype(_BF16)
    acc = x
    taps = wdw_ref.shape[0]
    n_chunks = d_ff // cw
    n_buf = ubuf_ref.shape[0]

    def up(c):
        for half in range(2):
            col = half * d_ff + c * cw
            buf = ubuf_ref.at[(2 * c + half) % n_buf]
            buf[0:SUBLANES, :] = carry_ref[:, col:col + cw]
            buf[SUBLANES:SUBLANES + tm, :] = jnp.dot(h, wup_ref[:, col:col + cw],
                                                     preferred_element_type=_F32)
            carry_ref[:, col:col + cw] = buf[tm:tm + SUBLANES, :]

    def tap_rows(c, half):
        col = half * d_ff + c * cw
        return _tap_rows(wdw_ref, col, cw)

    def conv(c, half, r0, w_rows):
        return _short_conv_tile(ubuf_ref.at[(2 * c + half) % n_buf], r0, rc, w_rows)

    up(0)
    for c in range(n_chunks):
        if c + 1 < n_chunks:
            up(c + 1)
        w_gate, w_lin = tap_rows(c, 0), tap_rows(c, 1)
        for r0 in range(0, tm, rc):
            act_ref[r0:r0 + rc, c * cw:(c + 1) * cw] = (
                _silu(conv(c, 0, r0, w_gate)) * conv(c, 1, r0, w_lin)).astype(_BF16)
        if (c + 1) % dg == 0 or c + 1 == n_chunks:
            k0 = (c // dg) * dg * cw
            acc = acc + jnp.dot(act_ref[:, k0:(c + 1) * cw], wdown_ref[k0:(c + 1) * cw, :],
                                preferred_element_type=_F32)
    o_ref[...] = acc


def _conv_ffn(x, g, w_up, w_dw, w_down, layer, *, batch, tm=512, cw=256, rc=64, dg=4):
    m, d = x.shape
    seq = m // batch
    d_ff = w_down.shape[1]
    nt = seq // tm
    row = lambda b, t: (b * nt + t, 0)
    return pl.pallas_call(
        functools.partial(_ffn_kernel, tm=tm, d_ff=d_ff, cw=cw, rc=rc, dg=dg),
        grid=(batch, nt),
        in_specs=[pl.BlockSpec((tm, d), row), _resident((1, d)), _resident_layer(w_up, layer),
                  _resident(w_dw.shape), _resident_layer(w_down, layer)],
        out_specs=pl.BlockSpec((tm, d), row),
        out_shape=jax.ShapeDtypeStruct((m, d), _F32),
        scratch_shapes=[pltpu.VMEM((SUBLANES, 2 * d_ff), _F32),
                        pltpu.VMEM((4, tm + SUBLANES, cw), _F32),
                        pltpu.VMEM((tm, d_ff), _BF16)],
        compiler_params=_params(),
        name="conv_ffn",
    )(x, g.reshape(1, d), w_up, w_dw, w_down)


def _dwconv_tiles(ubuf_ref, w_ref, b_ref, y_ref, zbuf_ref, *, tm, halo, rc):
    taps, d = w_ref.shape
    base = halo - (taps - 1)
    slabs = -(-(base + taps) // SUBLANES)

    def body(i, carry):
        r0 = pl.multiple_of(i * rc, rc)
        for c0 in range(0, d, LANES):
            cols = slice(c0, c0 + LANES)
            acc = jnp.broadcast_to(b_ref[:, cols], (rc, LANES))
            for r in range(SUBLANES):
                rows = rc if r == 0 else rc + SUBLANES
                z = None
                for a in range(slabs):
                    k = SUBLANES * a + r - base
                    if 0 <= k < taps:
                        term = ubuf_ref[pl.ds(r0 + SUBLANES * a, rows), cols] * w_ref[k:k + 1, cols]
                        z = term if z is None else z + term
                if z is None:
                    continue
                if r == 0:
                    acc = acc + z
                else:
                    zbuf_ref[r, :, cols] = z
                    acc = acc + zbuf_ref[r, pl.ds(r, rc), cols]
            y_ref[pl.ds(r0, rc), cols] = acc
        return carry

    lax.fori_loop(0, tm // rc, body, 0)


def _conformer_kernel(x_ref, g_ref, win_ref, bin_ref, wdw_ref, bdw_ref, lng_ref, lnb_ref, wout_ref,
                      o_ref, ubuf_ref, y_ref, zbuf_ref, *, tm, halo, rc):
    t = pl.program_id(1)
    d = x_ref.shape[1]

    @pl.when(t == 0)
    def _():
        ubuf_ref[0:halo, :] = jnp.zeros((halo, d), _F32)

    x = x_ref[...]
    h = _rms_rows(x, g_ref[...]).astype(_BF16)
    val = jnp.dot(h, win_ref[:, 0:d], preferred_element_type=_F32) + bin_ref[:, 0:d]
    gate = jnp.dot(h, win_ref[:, d:2 * d], preferred_element_type=_F32) + bin_ref[:, d:2 * d]
    ubuf_ref[halo:halo + tm, :] = val * jax.nn.sigmoid(gate)

    _dwconv_tiles(ubuf_ref, wdw_ref, bdw_ref, y_ref, zbuf_ref, tm=tm, halo=halo, rc=rc)
    ubuf_ref[0:halo, :] = ubuf_ref[tm:tm + halo, :]

    y = y_ref[...]
    yc = y - jnp.mean(y, axis=-1, keepdims=True)
    var = jnp.mean(yc * yc, axis=-1, keepdims=True)
    z = _silu(yc * lax.rsqrt(var + EPS) * lng_ref[...] + lnb_ref[...])
    o_ref[...] = x + jnp.dot(z.astype(_BF16), wout_ref[...], preferred_element_type=_F32)


def _conformer(x, g, w_in, b_in, w_dw, b_dw, ln_g, ln_b, w_out, layer, *, batch, tm=512, rc=64):
    m, d = x.shape
    seq = m // batch
    nt = seq // tm
    taps = w_dw.shape[0]
    halo = -(-(taps - 1) // SUBLANES) * SUBLANES
    row = lambda b, t: (b * nt + t, 0)
    return pl.pallas_call(
        functools.partial(_conformer_kernel, tm=tm, halo=halo, rc=rc),
        grid=(batch, nt),
        in_specs=[pl.BlockSpec((tm, d), row), _resident((1, d)), _resident_layer(w_in, layer),
                  _resident((1, 2 * d)), _resident(w_dw.shape), _resident((1, d)),
                  _resident((1, d)), _resident((1, d)), _resident_layer(w_out, layer)],
        out_specs=pl.BlockSpec((tm, d), row),
        out_shape=jax.ShapeDtypeStruct((m, d), _F32),
        scratch_shapes=[pltpu.VMEM((tm + halo, d), _F32), pltpu.VMEM((tm, d), _F32),
                        pltpu.VMEM((SUBLANES, rc + SUBLANES, d), _F32)],
        compiler_params=_params(),
        name="conformer",
    )(x, g.reshape(1, d), w_in, b_in.reshape(1, 2 * d), w_dw, b_dw.reshape(1, d),
      ln_g.reshape(1, d), ln_b.reshape(1, d), w_out)


def _out_proj_kernel(a_ref, w_ref, res_ref, o_ref):
    o_ref[...] = res_ref[...] + jnp.dot(a_ref[...], w_ref[...], preferred_element_type=_F32)


def _out_proj(a, w, layer, res, *, tm=1024):
    m, k = a.shape
    d = w.shape[2]
    return pl.pallas_call(
        _out_proj_kernel,
        grid=(m // tm,),
        in_specs=[pl.BlockSpec((tm, k), lambda i: (i, 0)), _resident_layer(w, layer),
                  pl.BlockSpec((tm, d), lambda i: (i, 0))],
        out_specs=pl.BlockSpec((tm, d), lambda i: (i, 0)),
        out_shape=jax.ShapeDtypeStruct((m, d), _F32),
        compiler_params=_params(1),
        name="out_proj",
    )(a, w, res)


def _gdn_in_kernel(x_ref, g_ref, w_ref, wab_ref, cw_ref, qkv_ref, z_ref, ab_ref, carry_ref, *ubuf_refs,
                   tm, cwid, n_conv, rc):
    t = pl.program_id(1)

    @pl.when(t == 0)
    def _():
        carry_ref[...] = jnp.zeros_like(carry_ref)

    h = _rms_rows(x_ref[...], g_ref[...]).astype(_BF16)
    n_chunks = n_conv // cwid
    n_buf = len(ubuf_refs)

    def up(c):
        col = c * cwid
        buf = ubuf_refs[c % n_buf]
        buf[0:SUBLANES, :] = carry_ref[:, col:col + cwid]
        buf[SUBLANES:SUBLANES + tm, :] = jnp.dot(h, w_ref[:, col:col + cwid], preferred_element_type=_F32)
        carry_ref[:, col:col + cwid] = buf[tm:tm + SUBLANES, :]

    up(0)
    for c in range(n_chunks):
        if c + 1 < n_chunks:
            up(c + 1)
        else:
            z_ref[...] = jnp.dot(h, w_ref[:, n_conv:], preferred_element_type=_F32)
            ab_ref[...] = jnp.dot(h, wab_ref[...], preferred_element_type=_F32)
        w_rows = _tap_rows(cw_ref, c * cwid, cwid)
        for r0 in range(0, tm, rc):
            qkv_ref[r0:r0 + rc, c * cwid:(c + 1) * cwid] = _silu(
                _short_conv_tile(ubuf_refs[c % n_buf], r0, rc, w_rows))


def _gdn_in(x, g, w_in, layer, conv_w, *, batch, tm=512, cwid=256, rc=64):
    m, d = x.shape
    seq = m // batch
    nt = seq // tm
    n_conv = conv_w.shape[1]
    n_main = n_conv + d
    n_ab = w_in.shape[2] - n_main
    w_ab = jnp.pad(w_in[layer, :, n_main:], ((0, 0), (0, LANES - n_ab)))
    row = lambda b, t: (b * nt + t, 0)
    return pl.pallas_call(
        functools.partial(_gdn_in_kernel, tm=tm, cwid=cwid, n_conv=n_conv, rc=rc),
        grid=(batch, nt),
        in_specs=[pl.BlockSpec((tm, d), row), _resident((1, d)), _resident_layer(w_in, layer, n_main),
                  _resident(w_ab.shape), _resident(conv_w.shape)],
        out_specs=[pl.BlockSpec((tm, n_conv), row), pl.BlockSpec((tm, d), row),
                   pl.BlockSpec((tm, LANES), row)],
        out_shape=[jax.ShapeDtypeStruct((m, n_conv), _F32), jax.ShapeDtypeStruct((m, d), _F32),
                   jax.ShapeDtypeStruct((m, LANES), _F32)],
        scratch_shapes=[pltpu.VMEM((SUBLANES, n_conv), _F32),
                        *[pltpu.VMEM((tm + SUBLANES, cwid), _F32)] * 3],
        compiler_params=_params(),
        name="gdn_in",
    )(x, g.reshape(1, d), w_in, w_ab, conv_w)


def _gdn_chunk_kernel(qkv_ref, z_ref, ab_ref, alog_ref, dtb_ref, ong_ref, o_ref, state_ref, *,
                      ts, heads, group):
    t = pl.program_id(1)
    C, Dh = GDN_CHUNK, HEAD_DIM
    W = heads * Dh
    shift = C.bit_length() - 1

    @pl.when(t == 0)
    def _():
        state_ref[...] = jnp.zeros_like(state_ref)

    ri = lax.broadcasted_iota(jnp.int32, (ts, ts), 0)
    ci = lax.broadcasted_iota(jnp.int32, (ts, ts), 1)
    same = (ri >> shift) == (ci >> shift)
    dist = jnp.where(same, ri - ci, -1)
    lower = dist >= 0
    strict = dist > 0
    eye = (dist == 0).astype(_F32)
    sel = (lax.broadcasted_iota(jnp.int32, (SUBLANES, LANES), 0)
           == lax.broadcasted_iota(jnp.int32, (SUBLANES, LANES), 1)).astype(_F32)

    ab = ab_ref[...]
    g_raw = -jnp.exp(alog_ref[...]) * jax.nn.softplus(ab + dtb_ref[...])
    beta_all = jax.nn.sigmoid(ab)
    gc = _dot_sel(lower, g_raw)
    gtot = _dot_sel(same, g_raw)
    gc_rows = _dot_sel(sel, gc, transpose_x=True)
    eg_all = jnp.exp(gc)
    egl_all = jnp.exp(gtot)
    ekd_all = jnp.exp(gtot - gc)
    ong = ong_ref[...]

    def head(hd):
        lo = hd * Dh
        q = qkv_ref[:, lo:lo + Dh]
        k = qkv_ref[:, W + lo:W + lo + Dh]
        v = qkv_ref[:, 2 * W + lo:2 * W + lo + Dh]
        q = q * lax.rsqrt(jnp.sum(q * q, axis=-1, keepdims=True) + EPS) * (Dh ** -0.5)
        k = k * lax.rsqrt(jnp.sum(k * k, axis=-1, keepdims=True) + EPS)
        gcol = gc[:, hd:hd + 1]
        grow = gc_rows[hd:hd + 1, :]
        eg = eg_all[:, hd:hd + 1]
        beta = beta_all[:, heads + hd:heads + hd + 1]
        kb = k * beta
        vb = v * beta
        decay = jnp.where(lower, jnp.exp(jnp.where(lower, gcol - grow, 0.0)), 0.0)
        a_mat = jnp.where(strict, _dot_nt(kb, k) * decay, 0.0)
        qk = jnp.where(lower, _dot_nt(q, k) * decay, 0.0)
        yield
        t_mat = eye - a_mat
        pw = a_mat
        for _ in range(shift - 1):
            pw = _dot(pw, pw)
            yield
            t_mat = t_mat + _dot(t_mat, pw)
        yield
        uw = _dot(t_mat, jnp.concatenate([vb, kb * eg], axis=1))
        yield
        u = uw[:, :Dh]
        w = uw[:, Dh:]
        qg = q * eg
        kd = k * ekd_all[:, hd:hd + 1]
        s = state_ref[hd]
        v_new, o_state = [], []
        for n in range(ts // C):
            r = slice(n * C, (n + 1) * C)
            ws = _dot(jnp.concatenate([w[r], qg[r]], axis=0), s)
            yield
            vn = u[r] - ws[:C]
            v_new.append(vn)
            o_state.append(ws[C:])
            s = s * egl_all[n * C:n * C + 1, hd:hd + 1] + _dot_tn(kd[r], vn)
            yield
        state_ref[hd] = s
        o = jnp.concatenate(o_state, axis=0) + _dot(qk, jnp.concatenate(v_new, axis=0))
        o = _rms_rows(o, ong) * _silu(z_ref[:, lo:lo + Dh])
        o_ref[:, lo:lo + Dh] = o.astype(o_ref.dtype)

    for h0 in range(0, heads, group):
        _round_robin(head(hd) for hd in range(h0, min(h0 + group, heads)))


def _gdn_chunk(qkv, z, ab, a_log, dt_bias, o_norm_g, *, batch, heads, ts=256, group=8):
    m, w3 = qkv.shape
    d = z.shape[1]
    seq = m // batch
    nt = seq // ts
    row = lambda b, t: (b * nt + t, 0)
    pad = lambda v: jnp.pad(v.reshape(1, -1), ((0, 0), (0, LANES - v.shape[0])))
    return pl.pallas_call(
        functools.partial(_gdn_chunk_kernel, ts=ts, heads=heads, group=group),
        grid=(batch, nt),
        in_specs=[pl.BlockSpec((ts, w3), row), pl.BlockSpec((ts, d), row),
                  pl.BlockSpec((ts, LANES), row), _resident((1, LANES)), _resident((1, LANES)),
                  _resident((1, HEAD_DIM))],
        out_specs=pl.BlockSpec((ts, d), row),
        out_shape=jax.ShapeDtypeStruct((m, d), _BF16),
        scratch_shapes=[pltpu.VMEM((heads, HEAD_DIM, HEAD_DIM), _F32)],
        compiler_params=_params(),
        name="gdn_chunk",
    )(qkv, z, ab, pad(a_log), pad(dt_bias), o_norm_g.reshape(1, HEAD_DIM))


def _gated_deltanet(x, g, w_in, conv_w, a_log, dt_bias, o_norm_g, w_out, layer, *, batch):
    heads = a_log.shape[0]
    qkv, z, ab = _gdn_in(x, g, w_in, layer, conv_w, batch=batch)
    o = _gdn_chunk(qkv, z, ab, a_log, dt_bias, o_norm_g, batch=batch, heads=heads)
    return _out_proj(o, w_out, layer, x)


def _fox_in_kernel(x_ref, g_ref, w_ref, wf_ref, bf_ref, qg_ref, kg_ref, q_ref, k_ref, v_ref, c_ref,
                   csum_ref, pbuf_ref, *, tm, heads, rc, cb):
    t = pl.program_id(1)
    Dh = HEAD_DIM
    W = heads * Dh

    @pl.when(t == 0)
    def _():
        csum_ref[...] = jnp.zeros_like(csum_ref)

    h = _rms_rows(x_ref[...], g_ref[...]).astype(_BF16)
    scale = Dh ** -0.5
    pw = 2 * Dh
    n_qk = 2 * W // pw
    n_buf = pbuf_ref.shape[0]

    def proj(c):
        pbuf_ref[c % n_buf] = jnp.dot(h, w_ref[:, c * pw:(c + 1) * pw], preferred_element_type=_F32)

    def cumsum_gates():
        f = jnp.dot(h, wf_ref[...], preferred_element_type=_F32) + bf_ref[...]
        log_f = jax.nn.log_sigmoid(f)
        ri = lax.broadcasted_iota(jnp.int32, (cb, cb), 0)
        ci = lax.broadcasted_iota(jnp.int32, (cb, cb), 1)
        tri = ri >= ci
        run = csum_ref[0:1, :]
        for r0 in range(0, tm, cb):
            c = _dot_sel(tri, log_f[r0:r0 + cb, :]) + run
            c_ref[r0:r0 + cb, :] = c
            run = c[cb - 1:cb, :]
        csum_ref[0:1, :] = run

    proj(0)
    for c in range(n_qk):
        if c + 1 < n_qk:
            proj(c + 1)
        else:
            v_ref[...] = jnp.dot(h, w_ref[:, 2 * W:3 * W],
                                 preferred_element_type=_F32).astype(v_ref.dtype)
            cumsum_gates()
        is_q = c < n_qk // 2
        out_ref = q_ref if is_q else k_ref
        gain = qg_ref[...] if is_q else kg_ref[...]
        col0 = (c % (n_qk // 2)) * pw
        for r0 in range(0, tm, rc):
            for lo in range(0, pw, Dh):
                y = _rms_rows(pbuf_ref[c % n_buf, r0:r0 + rc, lo:lo + Dh], gain)
                if is_q:
                    y = y * scale
                out_ref[r0:r0 + rc, col0 + lo:col0 + lo + Dh] = y.astype(out_ref.dtype)


def _fox_in(x, g, w_in, layer, b_f, q_norm_g, k_norm_g, *, batch, heads, tm=512, rc=64, cb=256):
    m, d = x.shape
    seq = m // batch
    nt = seq // tm
    W = heads * HEAD_DIM
    w_f = jnp.pad(w_in[layer, :, 3 * W:], ((0, 0), (0, LANES - heads)))
    b_pad = jnp.pad(b_f.reshape(1, heads), ((0, 0), (0, LANES - heads)))
    row = lambda b, t: (b * nt + t, 0)
    return pl.pallas_call(
        functools.partial(_fox_in_kernel, tm=tm, heads=heads, rc=rc, cb=min(cb, tm)),
        grid=(batch, nt),
        in_specs=[pl.BlockSpec((tm, d), row), _resident((1, d)), _resident_layer(w_in, layer, 3 * W),
                  _resident(w_f.shape), _resident((1, LANES)), _resident((1, HEAD_DIM)),
                  _resident((1, HEAD_DIM))],
        out_specs=[pl.BlockSpec((tm, W), row)] * 3 + [pl.BlockSpec((tm, LANES), row)],
        out_shape=[jax.ShapeDtypeStruct((m, W), _BF16)] * 3 + [jax.ShapeDtypeStruct((m, LANES), _F32)],
        scratch_shapes=[pltpu.VMEM((SUBLANES, LANES), _F32),
                        pltpu.VMEM((3, tm, 2 * HEAD_DIM), _F32)],
        compiler_params=_params(),
        name="fox_in",
    )(x, g.reshape(1, d), w_in, w_f, b_pad, q_norm_g.reshape(1, HEAD_DIM),
      k_norm_g.reshape(1, HEAD_DIM))


def _fox_attn_kernel(q_ref, k_ref, v_ref, ccol_ref, crow_ref, o_ref, m_ref, l_ref, acc_ref, cq_ref, *,
                     tq, heads):
    qi = pl.program_id(1)
    Dh = HEAD_DIM
    ri = lax.broadcasted_iota(jnp.int32, (tq, Dh), 0)
    ci = lax.broadcasted_iota(jnp.int32, (tq, Dh), 1)
    halves = tq // Dh
    causal = [ci + hf * Dh <= ri for hf in range(halves)]
    ones = jnp.ones((tq, Dh), _BF16)

    m_ref[...] = jnp.full(m_ref.shape, -jnp.inf, _F32)
    l_ref[...] = jnp.zeros(l_ref.shape, _F32)
    acc_ref[...] = jnp.zeros(acc_ref.shape, _F32)
    for hd in range(heads):
        cq_ref[hd] = jnp.broadcast_to(ccol_ref[:, hd:hd + 1], (tq, Dh))

    def head_step(hd, j, k0, masked):
        lo = hd * Dh
        s = _dot_nt(q_ref[:, lo:lo + Dh], k_ref[pl.ds(k0, tq), lo:lo + Dh])
        yield
        crow = crow_ref[hd, pl.ds(j, 1), :]
        parts = []
        for hf in range(halves):
            sh = s[:, hf * Dh:(hf + 1) * Dh] + cq_ref[hd] - crow[:, hf * Dh:(hf + 1) * Dh]
            parts.append(jnp.where(causal[hf], sh, -jnp.inf) if masked else sh)
        m_old = m_ref[hd]
        m_blk = functools.reduce(jnp.maximum, parts)
        m_new = jnp.maximum(m_old, jnp.broadcast_to(jnp.max(m_blk, axis=-1, keepdims=True), (tq, Dh)))
        yield
        alpha = jnp.exp(m_old - m_new)
        p = jnp.concatenate([jnp.exp(sh - m_new).astype(_BF16) for sh in parts], axis=1)
        yield
        pv = jnp.dot(p, jnp.concatenate([v_ref[pl.ds(k0, tq), lo:lo + Dh], ones], axis=1),
                     preferred_element_type=_F32)
        yield
        m_ref[hd] = m_new
        l_ref[hd] = alpha * l_ref[hd] + pv[:, Dh:]
        acc_ref[:, lo:lo + Dh] = alpha * acc_ref[:, lo:lo + Dh] + pv[:, :Dh]

    def step(j, masked):
        k0 = pl.multiple_of(j * tq, tq)
        _round_robin([head_step(hd, j, k0, masked) for hd in range(heads)])

    def body(j, carry):
        step(j, False)
        return carry

    lax.fori_loop(0, qi, body, 0)
    step(qi, True)
    for hd in range(heads):
        lo = hd * Dh
        o_ref[:, lo:lo + Dh] = (acc_ref[:, lo:lo + Dh] / l_ref[hd]).astype(o_ref.dtype)


def _fox_attn(q, k, v, c, *, batch, heads, tq=256):
    m, W = q.shape
    seq = m // batch
    nq = seq // tq
    c_rows = c[:, :heads].reshape(batch, seq, heads).transpose(0, 2, 1).reshape(batch, heads, nq, tq)
    row = lambda b, t: (b * nq + t, 0)
    return pl.pallas_call(
        functools.partial(_fox_attn_kernel, tq=tq, heads=heads),
        grid=(batch, nq),
        in_specs=[pl.BlockSpec((tq, W), row), pl.BlockSpec((seq, W), lambda b, t: (b, 0)),
                  pl.BlockSpec((seq, W), lambda b, t: (b, 0)), pl.BlockSpec((tq, LANES), row),
                  pl.BlockSpec((None, heads, nq, tq), lambda b, t: (b, 0, 0, 0))],
        out_specs=pl.BlockSpec((tq, W), row),
        out_shape=jax.ShapeDtypeStruct((m, W), _BF16),
        scratch_shapes=[pltpu.VMEM((heads, tq, HEAD_DIM), _F32), pltpu.VMEM((heads, tq, HEAD_DIM), _F32),
                        pltpu.VMEM((tq, W), _F32), pltpu.VMEM((heads, tq, HEAD_DIM), _F32)],
        compiler_params=_params(),
        name="fox_attn",
    )(q, k, v, c, c_rows)


def _forgetting_attention(x, g, w_in, b_f, q_norm_g, k_norm_g, w_out, layer, *, batch):
    heads = b_f.shape[0]
    q, k, v, c = _fox_in(x, g, w_in, layer, b_f, q_norm_g, k_norm_g, batch=batch, heads=heads)
    o = _fox_attn(q, k, v, c, batch=batch, heads=heads)
    return _out_proj(o, w_out, layer, x)


def kernel(x, mix_norm_g, ffn_norm_g, conv_w_in, conv_b_in, conv_w_dw, conv_b_dw, conv_ln_g, conv_ln_b,
           conv_w_out, gdn_w_in, gdn_conv_w, gdn_a_log, gdn_dt_bias, gdn_o_norm_g, gdn_w_out, fox_w_in,
           fox_b_f, fox_q_norm_g, fox_k_norm_g, fox_w_out, ffn_w_up, ffn_w_dw, ffn_w_down):
    batch, seq, d = x.shape
    depth = mix_norm_g.shape[0]
    xs = x.reshape(batch * seq, d)
    bf = lambda w: w.astype(_BF16)
    conv_w_in, conv_w_out, gdn_w_in, gdn_w_out = bf(conv_w_in), bf(conv_w_out), bf(gdn_w_in), bf(gdn_w_out)
    fox_w_in, fox_w_out, ffn_w_up, ffn_w_down = bf(fox_w_in), bf(fox_w_out), bf(ffn_w_up), bf(ffn_w_down)
    ia = ib = ic = 0
    for layer in range(depth):
        kind = layer % N_MIXERS
        g = mix_norm_g[layer]
        if kind == 0:
            xs = _conformer(xs, g, conv_w_in, conv_b_in[ia], conv_w_dw[ia], conv_b_dw[ia],
                            conv_ln_g[ia], conv_ln_b[ia], conv_w_out, ia, batch=batch)
            ia += 1
        elif kind == 1:
            xs = _gated_deltanet(xs, g, gdn_w_in, gdn_conv_w[ib], gdn_a_log[ib], gdn_dt_bias[ib],
                                 gdn_o_norm_g[ib], gdn_w_out, ib, batch=batch)
            ib += 1
        else:
            xs = _forgetting_attention(xs, g, fox_w_in, fox_b_f[ic], fox_q_norm_g[ic],
                                       fox_k_norm_g[ic], fox_w_out, ic, batch=batch)
            ic += 1
        xs = _conv_ffn(xs, ffn_norm_g[layer], ffn_w_up, ffn_w_dw[layer], ffn_w_down, layer, batch=batch)
    return xs.reshape(batch, seq, d)
```

```python
import functools

import jax
import jax.numpy as jnp
from jax import lax
from jax.experimental import pallas as pl
from jax.experimental.pallas import tpu as pltpu

EPS = 1e-6
LOG2_E = 1.4426950408889634
N_MIXERS = 3
HEAD_DIM = 128
GDN_CHUNK = 64
LANES = 128
SUBLANES = 8
VMEM_LIMIT = 56 * 1024 * 1024

_BF16 = jnp.bfloat16
_F32 = jnp.float32


def _dot(a, b):
    return jnp.dot(a.astype(_BF16), b.astype(_BF16), preferred_element_type=_F32)


def _dot_nt(a, b):
    return lax.dot_general(a.astype(_BF16), b.astype(_BF16), (((1,), (1,)), ((), ())),
                           preferred_element_type=_F32)


def _dot_tn(a, b):
    return lax.dot_general(a.astype(_BF16), b.astype(_BF16), (((0,), (0,)), ((), ())),
                           preferred_element_type=_F32)


def _dot_sel(sel, x, *, transpose_x=False):
    dims = (((1,), (1 if transpose_x else 0,)), ((), ()))
    hi = x.astype(_BF16)
    rem = x - hi.astype(_F32)
    mid = rem.astype(_BF16)
    lo = (rem - mid.astype(_F32)).astype(_BF16)
    s = sel.astype(_BF16)
    return sum(lax.dot_general(s, p, dims, preferred_element_type=_F32) for p in (hi, mid, lo))


def _rms_rows(x, g):
    return x * lax.rsqrt(jnp.mean(x * x, axis=-1, keepdims=True) + EPS) * g


def _silu(x):
    return x * jax.nn.sigmoid(x)


def _round_robin(gens):
    gens = list(gens)
    while gens:
        gens = [g for g in gens if next(g, True) is None]


def _tap_rows(w_ref, col, width):
    return [jnp.broadcast_to(w_ref[k:k + 1, col:col + width], (SUBLANES, width))
            for k in range(w_ref.shape[0])]


def _short_conv_tile(buf, r0, rc, w_rows):
    taps = len(w_rows)
    n = rc // SUBLANES
    groups = [buf[r0 + SUBLANES * i:r0 + SUBLANES * (i + 1), :] for i in range(n + 1)]
    sub = lax.broadcasted_iota(jnp.int32, groups[0].shape, 0)
    ys = None
    for k in range(taps):
        shift = taps - 1 - k
        if shift:
            rolled = [pltpu.roll(p, shift, axis=0) for p in groups]
            src = [jnp.where(sub < shift, rolled[i], rolled[i + 1]) for i in range(n)]
        else:
            src = groups[1:]
        terms = [p * w_rows[k] for p in src]
        ys = terms if ys is None else [a + b for a, b in zip(ys, terms)]
    return jnp.concatenate(ys, axis=0)


def _resident(shape):
    nd = len(shape)
    return pl.BlockSpec(shape, lambda *_: (0,) * nd, pipeline_mode=pl.Buffered(1))


def _resident_layer(stack, layer, width=None):
    _, r, c = stack.shape
    return pl.BlockSpec((None, r, width or c), lambda *_: (layer, 0, 0), pipeline_mode=pl.Buffered(1))


def _params(n_axes=2):
    sem = ("parallel",) + ("arbitrary",) * (n_axes - 1)
    return pltpu.CompilerParams(dimension_semantics=sem, vmem_limit_bytes=VMEM_LIMIT)


def _ffn_kernel(x_ref, g_ref, wup_ref, wdw_ref, wdown_ref, o_ref, carry_ref, ubuf_ref, act_ref, *,
                tm, d_ff, cw, rc, dg):
    t = pl.program_id(1)

    @pl.when(t == 0)
    def _():
        carry_ref[...] = jnp.zeros_like(carry_ref)

    x = x_ref[...]
    h = _rms_rows(x, g_ref[...]).astype(_BF16)
    acc = x
    taps = wdw_ref.shape[0]
    n_chunks = d_ff // cw
    n_buf = ubuf_ref.shape[0]

    def up(c):
        for half in range(2):
            col = half * d_ff + c * cw
            buf = ubuf_ref.at[(2 * c + half) % n_buf]
            buf[0:SUBLANES, :] = carry_ref[:, col:col + cw]
            buf[SUBLANES:SUBLANES + tm, :] = jnp.dot(h, wup_ref[:, col:col + cw],
                                                     preferred_element_type=_F32)
            carry_ref[:, col:col + cw] = buf[tm:tm + SUBLANES, :]

    def tap_rows(c, half):
        col = half * d_ff + c * cw
        return _tap_rows(wdw_ref, col, cw)

    def conv(c, half, r0, w_rows):
        return _short_conv_tile(ubuf_ref.at[(2 * c + half) % n_buf], r0, rc, w_rows)

    group_ends = set(range(dg - 1, n_chunks, dg)) | {n_chunks - 1}
    group_start = 0
    up(0)
    for c in range(n_chunks):
        if c + 1 < n_chunks:
            up(c + 1)
        w_gate, w_lin = tap_rows(c, 0), tap_rows(c, 1)
        for r0 in range(0, tm, rc):
            act_ref[r0:r0 + rc, c * cw:(c + 1) * cw] = (
                _silu(conv(c, 0, r0, w_gate)) * conv(c, 1, r0, w_lin)).astype(_BF16)
        if c in group_ends:
            k0, k1 = group_start * cw, (c + 1) * cw
            acc = acc + jnp.dot(act_ref[:, k0:k1], wdown_ref[k0:k1, :], preferred_element_type=_F32)
            group_start = c + 1
    o_ref[...] = acc


def _conv_ffn(x, g, w_up, w_dw, w_down, layer, *, batch, tm=1024, cw=256, rc=64, dg=4):
    m, d = x.shape
    seq = m // batch
    d_ff = w_down.shape[1]
    nt = seq // tm
    row = lambda b, t: (b * nt + t, 0)
    return pl.pallas_call(
        functools.partial(_ffn_kernel, tm=tm, d_ff=d_ff, cw=cw, rc=rc, dg=dg),
        grid=(batch, nt),
        in_specs=[pl.BlockSpec((tm, d), row), _resident((1, d)), _resident_layer(w_up, layer),
                  _resident(w_dw.shape), _resident_layer(w_down, layer)],
        out_specs=pl.BlockSpec((tm, d), row),
        out_shape=jax.ShapeDtypeStruct((m, d), _F32),
        scratch_shapes=[pltpu.VMEM((SUBLANES, 2 * d_ff), _F32),
                        pltpu.VMEM((4, tm + SUBLANES, cw), _F32),
                        pltpu.VMEM((tm, d_ff), _BF16)],
        compiler_params=_params(),
        name="conv_ffn",
    )(x, g.reshape(1, d), w_up, w_dw, w_down)


def _dwconv_tiles(ubuf_ref, w_ref, b_ref, y_ref, *, tm, halo, rc):
    taps, d = w_ref.shape
    base = halo - (taps - 1)
    slabs = -(-(base + taps) // SUBLANES)
    n_out = rc // SUBLANES
    sub = lax.broadcasted_iota(jnp.int32, (SUBLANES, LANES), 0)

    def body(i, carry):
        r0 = pl.multiple_of(i * rc, rc)
        for c0 in range(0, d, LANES):
            cols = slice(c0, c0 + LANES)
            grp = [ubuf_ref[pl.ds(r0 + SUBLANES * j, SUBLANES), cols] for j in range(n_out + slabs - 1)]
            bias = jnp.broadcast_to(b_ref[:, cols], (SUBLANES, LANES))
            acc = [bias] * n_out
            for r in range(SUBLANES):
                ks = [(a, SUBLANES * a + r - base) for a in range(slabs)
                      if 0 <= SUBLANES * a + r - base < taps]
                if not ks:
                    continue
                n_z = n_out if r == 0 else n_out + 1
                z = []
                for j in range(n_z):
                    terms = [grp[j + a] * w_ref[k:k + 1, cols] for a, k in ks]
                    z.append(functools.reduce(lambda p, q: p + q, terms))
                if r == 0:
                    acc = [p + q for p, q in zip(acc, z)]
                else:
                    rolled = [pltpu.roll(zj, SUBLANES - r, axis=0) for zj in z]
                    acc = [p + jnp.where(sub < SUBLANES - r, rolled[j], rolled[j + 1])
                           for j, p in enumerate(acc)]
            y_ref[pl.ds(r0, rc), cols] = jnp.concatenate(acc, axis=0)
        return carry

    lax.fori_loop(0, tm // rc, body, 0)


def _conformer_kernel(x_ref, g_ref, win_ref, bin_ref, wdw_ref, bdw_ref, lng_ref, lnb_ref, wout_ref,
                      o_ref, ubuf_ref, y_ref, *, tm, halo, rc):
    t = pl.program_id(1)
    d = x_ref.shape[1]

    @pl.when(t == 0)
    def _():
        ubuf_ref[0:halo, :] = jnp.zeros((halo, d), _F32)

    x = x_ref[...]
    h = _rms_rows(x, g_ref[...]).astype(_BF16)
    val = jnp.dot(h, win_ref[:, 0:d], preferred_element_type=_F32) + bin_ref[:, 0:d]
    gate = jnp.dot(h, win_ref[:, d:2 * d], preferred_element_type=_F32) + bin_ref[:, d:2 * d]
    ubuf_ref[halo:halo + tm, :] = val * jax.nn.sigmoid(gate)

    _dwconv_tiles(ubuf_ref, wdw_ref, bdw_ref, y_ref, tm=tm, halo=halo, rc=rc)
    ubuf_ref[0:halo, :] = ubuf_ref[tm:tm + halo, :]

    y = y_ref[...]
    yc = y - jnp.mean(y, axis=-1, keepdims=True)
    var = jnp.mean(yc * yc, axis=-1, keepdims=True)
    z = _silu(yc * lax.rsqrt(var + EPS) * lng_ref[...] + lnb_ref[...])
    o_ref[...] = x + jnp.dot(z.astype(_BF16), wout_ref[...], preferred_element_type=_F32)


def _conformer(x, g, w_in, b_in, w_dw, b_dw, ln_g, ln_b, w_out, layer, *, batch, tm=512, rc=64):
    m, d = x.shape
    seq = m // batch
    nt = seq // tm
    taps = w_dw.shape[0]
    halo = -(-(taps - 1) // SUBLANES) * SUBLANES
    row = lambda b, t: (b * nt + t, 0)
    return pl.pallas_call(
        functools.partial(_conformer_kernel, tm=tm, halo=halo, rc=rc),
        grid=(batch, nt),
        in_specs=[pl.BlockSpec((tm, d), row), _resident((1, d)), _resident_layer(w_in, layer),
                  _resident((1, 2 * d)), _resident(w_dw.shape), _resident((1, d)),
                  _resident((1, d)), _resident((1, d)), _resident_layer(w_out, layer)],
        out_specs=pl.BlockSpec((tm, d), row),
        out_shape=jax.ShapeDtypeStruct((m, d), _F32),
        scratch_shapes=[pltpu.VMEM((tm + halo, d), _F32), pltpu.VMEM((tm, d), _F32)],
        compiler_params=_params(),
        name="conformer",
    )(x, g.reshape(1, d), w_in, b_in.reshape(1, 2 * d), w_dw, b_dw.reshape(1, d),
      ln_g.reshape(1, d), ln_b.reshape(1, d), w_out)


def _out_proj_kernel(a_ref, w_ref, res_ref, o_ref):
    o_ref[...] = res_ref[...] + jnp.dot(a_ref[...], w_ref[...], preferred_element_type=_F32)


def _out_proj(a, w, layer, res, *, tm=1024):
    m, k = a.shape
    d = w.shape[2]
    return pl.pallas_call(
        _out_proj_kernel,
        grid=(m // tm,),
        in_specs=[pl.BlockSpec((tm, k), lambda i: (i, 0)), _resident_layer(w, layer),
                  pl.BlockSpec((tm, d), lambda i: (i, 0))],
        out_specs=pl.BlockSpec((tm, d), lambda i: (i, 0)),
        out_shape=jax.ShapeDtypeStruct((m, d), _F32),
        compiler_params=_params(1),
        name="out_proj",
    )(a, w, res)


def _gdn_in_kernel(x_ref, g_ref, w_ref, wab_ref, cw_ref, qkv_ref, z_ref, ab_ref, carry_ref, *ubuf_refs,
                   tm, cwid, n_conv, rc):
    t = pl.program_id(1)

    @pl.when(t == 0)
    def _():
        carry_ref[...] = jnp.zeros_like(carry_ref)

    h = _rms_rows(x_ref[...], g_ref[...]).astype(_BF16)
    n_chunks = n_conv // cwid
    n_buf = len(ubuf_refs)

    def up(c):
        col = c * cwid
        buf = ubuf_refs[c % n_buf]
        buf[0:SUBLANES, :] = carry_ref[:, col:col + cwid]
        buf[SUBLANES:SUBLANES + tm, :] = jnp.dot(h, w_ref[:, col:col + cwid], preferred_element_type=_F32)
        carry_ref[:, col:col + cwid] = buf[tm:tm + SUBLANES, :]

    up(0)
    for c in range(n_chunks):
        if c + 1 < n_chunks:
            up(c + 1)
        else:
            z_ref[...] = jnp.dot(h, w_ref[:, n_conv:], preferred_element_type=_F32)
            ab_ref[...] = jnp.dot(h, wab_ref[...], preferred_element_type=_F32)
        w_rows = _tap_rows(cw_ref, c * cwid, cwid)
        for r0 in range(0, tm, rc):
            qkv_ref[r0:r0 + rc, c * cwid:(c + 1) * cwid] = _silu(
                _short_conv_tile(ubuf_refs[c % n_buf], r0, rc, w_rows))


def _gdn_in(x, g, w_in, layer, conv_w, *, batch, tm=512, cwid=256, rc=64):
    m, d = x.shape
    seq = m // batch
    nt = seq // tm
    n_conv = conv_w.shape[1]
    n_main = n_conv + d
    n_ab = w_in.shape[2] - n_main
    w_ab = jnp.pad(w_in[layer, :, n_main:], ((0, 0), (0, LANES - n_ab)))
    row = lambda b, t: (b * nt + t, 0)
    return pl.pallas_call(
        functools.partial(_gdn_in_kernel, tm=tm, cwid=cwid, n_conv=n_conv, rc=rc),
        grid=(batch, nt),
        in_specs=[pl.BlockSpec((tm, d), row), _resident((1, d)), _resident_layer(w_in, layer, n_main),
                  _resident(w_ab.shape), _resident(conv_w.shape)],
        out_specs=[pl.BlockSpec((tm, n_conv), row), pl.BlockSpec((tm, d), row),
                   pl.BlockSpec((tm, LANES), row)],
        out_shape=[jax.ShapeDtypeStruct((m, n_conv), _F32), jax.ShapeDtypeStruct((m, d), _F32),
                   jax.ShapeDtypeStruct((m, LANES), _F32)],
        scratch_shapes=[pltpu.VMEM((SUBLANES, n_conv), _F32),
                        *[pltpu.VMEM((tm + SUBLANES, cwid), _F32)] * 2],
        compiler_params=_params(),
        name="gdn_in",
    )(x, g.reshape(1, d), w_in, w_ab, conv_w)


def _gdn_chunk_kernel(qkv_ref, z_ref, ab_ref, alog_ref, dtb_ref, ong_ref, o_ref, state_ref, *,
                      ts, heads, group):
    t = pl.program_id(1)
    C, Dh = GDN_CHUNK, HEAD_DIM
    W = heads * Dh
    shift = C.bit_length() - 1

    @pl.when(t == 0)
    def _():
        state_ref[...] = jnp.zeros_like(state_ref)

    ri = lax.broadcasted_iota(jnp.int32, (ts, ts), 0)
    ci = lax.broadcasted_iota(jnp.int32, (ts, ts), 1)
    same = (ri >> shift) == (ci >> shift)
    dist = jnp.where(same, ri - ci, -1)
    lower = dist >= 0
    strict = dist > 0
    eye = (dist == 0).astype(_F32)
    sel = (lax.broadcasted_iota(jnp.int32, (SUBLANES, LANES), 0)
           == lax.broadcasted_iota(jnp.int32, (SUBLANES, LANES), 1)).astype(_F32)

    ab = ab_ref[...]
    g_raw = -jnp.exp(alog_ref[...]) * jax.nn.softplus(ab + dtb_ref[...])
    beta_all = jax.nn.sigmoid(ab)
    gc = _dot_sel(lower, g_raw)
    gtot = _dot_sel(same, g_raw)
    gc_rows = _dot_sel(sel, gc, transpose_x=True)
    eg_all = jnp.exp(gc)
    egl_all = jnp.exp(gtot)
    ekd_all = jnp.exp(gtot - gc)
    ong = ong_ref[...]

    def head(hd):
        lo = hd * Dh
        q = qkv_ref[:, lo:lo + Dh]
        k = qkv_ref[:, W + lo:W + lo + Dh]
        v = qkv_ref[:, 2 * W + lo:2 * W + lo + Dh]
        q = q * lax.rsqrt(jnp.sum(q * q, axis=-1, keepdims=True) + EPS) * (Dh ** -0.5)
        k = k * lax.rsqrt(jnp.sum(k * k, axis=-1, keepdims=True) + EPS)
        gcol = gc[:, hd:hd + 1]
        grow = gc_rows[hd:hd + 1, :]
        eg = eg_all[:, hd:hd + 1]
        beta = beta_all[:, heads + hd:heads + hd + 1]
        kb = k * beta
        vb = v * beta
        decay = jnp.where(lower, jnp.exp(jnp.where(lower, gcol - grow, 0.0)), 0.0)
        a_mat = jnp.where(strict, _dot_nt(kb, k) * decay, 0.0)
        qk = jnp.where(lower, _dot_nt(q, k) * decay, 0.0)
        yield
        t_mat = eye - a_mat
        pw = a_mat
        for _ in range(shift - 1):
            pw = _dot(pw, pw)
            yield
            t_mat = t_mat + _dot(t_mat, pw)
        yield
        uw = _dot(t_mat, jnp.concatenate([vb, kb * eg], axis=1))
        yield
        u = uw[:, :Dh]
        w = uw[:, Dh:]
        qg = q * eg
        kd = k * ekd_all[:, hd:hd + 1]
        s = state_ref[hd]
        v_new, o_state = [], []
        for n in range(ts // C):
            r = slice(n * C, (n + 1) * C)
            ws = _dot(jnp.concatenate([w[r], qg[r]], axis=0), s)
            yield
            vn = u[r] - ws[:C]
            v_new.append(vn)
            o_state.append(ws[C:])
            s = s * egl_all[n * C:n * C + 1, hd:hd + 1] + _dot_tn(kd[r], vn)
            yield
        state_ref[hd] = s
        o = jnp.concatenate(o_state, axis=0) + _dot(qk, jnp.concatenate(v_new, axis=0))
        o = _rms_rows(o, ong) * _silu(z_ref[:, lo:lo + Dh])
        o_ref[:, lo:lo + Dh] = o.astype(o_ref.dtype)

    for h0 in range(0, heads, group):
        _round_robin(head(hd) for hd in range(h0, min(h0 + group, heads)))


def _gdn_chunk(qkv, z, ab, a_log, dt_bias, o_norm_g, *, batch, heads, ts=256, group=8):
    m, w3 = qkv.shape
    d = z.shape[1]
    seq = m // batch
    nt = seq // ts
    row = lambda b, t: (b * nt + t, 0)
    pad = lambda v: jnp.pad(v.reshape(1, -1), ((0, 0), (0, LANES - v.shape[0])))
    return pl.pallas_call(
        functools.partial(_gdn_chunk_kernel, ts=ts, heads=heads, group=group),
        grid=(batch, nt),
        in_specs=[pl.BlockSpec((ts, w3), row), pl.BlockSpec((ts, d), row),
                  pl.BlockSpec((ts, LANES), row), _resident((1, LANES)), _resident((1, LANES)),
                  _resident((1, HEAD_DIM))],
        out_specs=pl.BlockSpec((ts, d), row),
        out_shape=jax.ShapeDtypeStruct((m, d), _BF16),
        scratch_shapes=[pltpu.VMEM((heads, HEAD_DIM, HEAD_DIM), _F32)],
        compiler_params=_params(),
        name="gdn_chunk",
    )(qkv, z, ab, pad(a_log), pad(dt_bias), o_norm_g.reshape(1, HEAD_DIM))


def _gated_deltanet(x, g, w_in, conv_w, a_log, dt_bias, o_norm_g, w_out, layer, *, batch):
    heads = a_log.shape[0]
    qkv, z, ab = _gdn_in(x, g, w_in, layer, conv_w, batch=batch)
    o = _gdn_chunk(qkv, z, ab, a_log, dt_bias, o_norm_g, batch=batch, heads=heads)
    return _out_proj(o, w_out, layer, x)


def _fox_in_kernel(x_ref, g_ref, w_ref, wf_ref, bf_ref, qg_ref, kg_ref, q_ref, k_ref, v_ref, c_ref,
                   csum_ref, pbuf_ref, *, tm, heads, rc, cb):
    t = pl.program_id(1)
    Dh = HEAD_DIM
    W = heads * Dh

    @pl.when(t == 0)
    def _():
        csum_ref[...] = jnp.zeros_like(csum_ref)

    h = _rms_rows(x_ref[...], g_ref[...]).astype(_BF16)
    scale = Dh ** -0.5 * LOG2_E
    pw = 2 * Dh
    n_qk = 2 * W // pw
    n_buf = pbuf_ref.shape[0]

    def proj(c):
        pbuf_ref[c % n_buf] = jnp.dot(h, w_ref[:, c * pw:(c + 1) * pw], preferred_element_type=_F32)

    def cumsum_gates():
        f = jnp.dot(h, wf_ref[...], preferred_element_type=_F32) + bf_ref[...]
        log_f = jax.nn.log_sigmoid(f) * LOG2_E
        ri = lax.broadcasted_iota(jnp.int32, (cb, cb), 0)
        ci = lax.broadcasted_iota(jnp.int32, (cb, cb), 1)
        tri = ri >= ci
        run = csum_ref[0:1, :]
        for r0 in range(0, tm, cb):
            c = _dot_sel(tri, log_f[r0:r0 + cb, :]) + run
            c_ref[r0:r0 + cb, :] = c
            run = c[cb - 1:cb, :]
        csum_ref[0:1, :] = run

    proj(0)
    for c in range(n_qk):
        if c + 1 < n_qk:
            proj(c + 1)
        else:
            v_ref[...] = jnp.dot(h, w_ref[:, 2 * W:3 * W],
                                 preferred_element_type=_F32).astype(v_ref.dtype)
            cumsum_gates()
        is_q = c < n_qk // 2
        out_ref = q_ref if is_q else k_ref
        gain = qg_ref[...] if is_q else kg_ref[...]
        col0 = (c % (n_qk // 2)) * pw
        for r0 in range(0, tm, rc):
            for lo in range(0, pw, Dh):
                y = _rms_rows(pbuf_ref[c % n_buf, r0:r0 + rc, lo:lo + Dh], gain)
                if is_q:
                    y = y * scale
                out_ref[r0:r0 + rc, col0 + lo:col0 + lo + Dh] = y.astype(out_ref.dtype)


def _fox_in(x, g, w_in, layer, b_f, q_norm_g, k_norm_g, *, batch, heads, tm=512, rc=64, cb=256):
    m, d = x.shape
    seq = m // batch
    nt = seq // tm
    W = heads * HEAD_DIM
    w_f = jnp.pad(w_in[layer, :, 3 * W:], ((0, 0), (0, LANES - heads)))
    b_pad = jnp.pad(b_f.reshape(1, heads), ((0, 0), (0, LANES - heads)))
    row = lambda b, t: (b * nt + t, 0)
    return pl.pallas_call(
        functools.partial(_fox_in_kernel, tm=tm, heads=heads, rc=rc, cb=min(cb, tm)),
        grid=(batch, nt),
        in_specs=[pl.BlockSpec((tm, d), row), _resident((1, d)), _resident_layer(w_in, layer, 3 * W),
                  _resident(w_f.shape), _resident((1, LANES)), _resident((1, HEAD_DIM)),
                  _resident((1, HEAD_DIM))],
        out_specs=[pl.BlockSpec((tm, W), row)] * 3 + [pl.BlockSpec((tm, LANES), row)],
        out_shape=[jax.ShapeDtypeStruct((m, W), _BF16)] * 3 + [jax.ShapeDtypeStruct((m, LANES), _F32)],
        scratch_shapes=[pltpu.VMEM((SUBLANES, LANES), _F32),
                        pltpu.VMEM((3, tm, 2 * HEAD_DIM), _F32)],
        compiler_params=_params(),
        name="fox_in",
    )(x, g.reshape(1, d), w_in, w_f, b_pad, q_norm_g.reshape(1, HEAD_DIM),
      k_norm_g.reshape(1, HEAD_DIM))


def _fox_attn_kernel(q_ref, k_ref, v_ref, ccol_ref, crow_ref, o_ref, m_ref, l_ref, acc_ref, cq_ref, *,
                     tq, heads):
    qi = pl.program_id(1)
    Dh = HEAD_DIM
    ri = lax.broadcasted_iota(jnp.int32, (tq, Dh), 0)
    ci = lax.broadcasted_iota(jnp.int32, (tq, Dh), 1)
    halves = tq // Dh
    causal = [ci + hf * Dh <= ri for hf in range(halves)]
    ones = jnp.ones((tq, Dh), _BF16)

    m_ref[...] = jnp.full(m_ref.shape, -jnp.inf, _F32)
    l_ref[...] = jnp.zeros(l_ref.shape, _F32)
    acc_ref[...] = jnp.zeros(acc_ref.shape, _F32)
    for hd in range(heads):
        cq_ref[hd] = jnp.broadcast_to(ccol_ref[:, hd:hd + 1], (tq, Dh))

    def head_step(hd, j, k0, masked):
        lo = hd * Dh
        s = _dot_nt(q_ref[:, lo:lo + Dh], k_ref[pl.ds(k0, tq), lo:lo + Dh])
        yield
        crow = crow_ref[hd, pl.ds(j, 1), :]
        parts = []
        for hf in range(halves):
            sh = s[:, hf * Dh:(hf + 1) * Dh] - crow[:, hf * Dh:(hf + 1) * Dh]
            parts.append(jnp.where(causal[hf], sh, -jnp.inf) if masked else sh)
        cq = cq_ref[hd]
        m_old = m_ref[hd]
        m_blk = jnp.max(functools.reduce(jnp.maximum, parts), axis=-1, keepdims=True)
        m_new = jnp.maximum(m_old, jnp.broadcast_to(m_blk, (tq, Dh)) + cq)
        yield
        alpha = jnp.exp2(m_old - m_new)
        row = cq - m_new
        p = jnp.concatenate([jnp.exp2(sh + row).astype(_BF16) for sh in parts], axis=1)
        yield
        pv = jnp.dot(p, jnp.concatenate([v_ref[pl.ds(k0, tq), lo:lo + Dh], ones], axis=1),
                     preferred_element_type=_F32)
        yield
        m_ref[hd] = m_new
        l_ref[hd] = alpha * l_ref[hd] + pv[:, Dh:]
        acc_ref[:, lo:lo + Dh] = alpha * acc_ref[:, lo:lo + Dh] + pv[:, :Dh]

    def step(j, masked):
        k0 = pl.multiple_of(j * tq, tq)
        _round_robin([head_step(hd, j, k0, masked) for hd in range(heads)])

    def body(j, carry):
        step(j, False)
        return carry

    lax.fori_loop(0, qi, body, 0)
    step(qi, True)
    for hd in range(heads):
        lo = hd * Dh
        o_ref[:, lo:lo + Dh] = (acc_ref[:, lo:lo + Dh] / l_ref[hd]).astype(o_ref.dtype)


def _fox_attn(q, k, v, c, *, batch, heads, tq=256):
    m, W = q.shape
    seq = m // batch
    nq = seq // tq
    c_rows = c[:, :heads].reshape(batch, seq, heads).transpose(0, 2, 1).reshape(batch, heads, nq, tq)
    row = lambda b, t: (b * nq + t, 0)
    return pl.pallas_call(
        functools.partial(_fox_attn_kernel, tq=tq, heads=heads),
        grid=(batch, nq),
        in_specs=[pl.BlockSpec((tq, W), row), pl.BlockSpec((seq, W), lambda b, t: (b, 0)),
                  pl.BlockSpec((seq, W), lambda b, t: (b, 0)), pl.BlockSpec((tq, LANES), row),
                  pl.BlockSpec((None, heads, nq, tq), lambda b, t: (b, 0, 0, 0))],
        out_specs=pl.BlockSpec((tq, W), row),
        out_shape=jax.ShapeDtypeStruct((m, W), _BF16),
        scratch_shapes=[pltpu.VMEM((heads, tq, HEAD_DIM), _F32), pltpu.VMEM((heads, tq, HEAD_DIM), _F32),
                        pltpu.VMEM((tq, W), _F32), pltpu.VMEM((heads, tq, HEAD_DIM), _F32)],
        compiler_params=_params(),
        name="fox_attn",
    )(q, k, v, c, c_rows)


def _forgetting_attention(x, g, w_in, b_f, q_norm_g, k_norm_g, w_out, layer, *, batch):
    heads = b_f.shape[0]
    q, k, v, c = _fox_in(x, g, w_in, layer, b_f, q_norm_g, k_norm_g, batch=batch, heads=heads)
    o = _fox_attn(q, k, v, c, batch=batch, heads=heads)
    return _out_proj(o, w_out, layer, x)


def kernel(x, mix_norm_g, ffn_norm_g, conv_w_in, conv_b_in, conv_w_dw, conv_b_dw, conv_ln_g, conv_ln_b,
           conv_w_out, gdn_w_in, gdn_conv_w, gdn_a_log, gdn_dt_bias, gdn_o_norm_g, gdn_w_out, fox_w_in,
           fox_b_f, fox_q_norm_g, fox_k_norm_g, fox_w_out, ffn_w_up, ffn_w_dw, ffn_w_down):
    batch, seq, d = x.shape
    depth = mix_norm_g.shape[0]
    xs = x.reshape(batch * seq, d)
    bf = lambda w: w.astype(_BF16)
    conv_w_in, conv_w_out, gdn_w_in, gdn_w_out = bf(conv_w_in), bf(conv_w_out), bf(gdn_w_in), bf(gdn_w_out)
    fox_w_in, fox_w_out, ffn_w_up, ffn_w_down = bf(fox_w_in), bf(fox_w_out), bf(ffn_w_up), bf(ffn_w_down)
    ia = ib = ic = 0
    for layer in range(depth):
        kind = layer % N_MIXERS
        g = mix_norm_g[layer]
        if kind == 0:
            xs = _conformer(xs, g, conv_w_in, conv_b_in[ia], conv_w_dw[ia], conv_b_dw[ia],
                            conv_ln_g[ia], conv_ln_b[ia], conv_w_out, ia, batch=batch)
            ia += 1
        elif kind == 1:
            xs = _gated_deltanet(xs, g, gdn_w_in, gdn_conv_w[ib], gdn_a_log[ib], gdn_dt_bias[ib],
                                 gdn_o_norm_g[ib], gdn_w_out, ib, batch=batch)
            ib += 1
        else:
            xs = _forgetting_attention(xs, g, fox_w_in, fox_b_f[ic], fox_q_norm_g[ic],
                                       fox_k_norm_g[ic], fox_w_out, ic, batch=batch)
            ic += 1
        xs = _conv_ffn(xs, ffn_norm_g[layer], ffn_w_up, ffn_w_dw[layer], ffn_w_down, layer, batch=batch)
    return xs.reshape(batch, seq, d)
```

```python
import functools

import jax
import jax.numpy as jnp
from jax import lax
from jax.experimental import pallas as pl
from jax.experimental.pallas import tpu as pltpu

EPS = 1e-6
LOG2_E = 1.4426950408889634
N_MIXERS = 3
HEAD_DIM = 128
GDN_CHUNK = 64
LANES = 128
SUBLANES = 8
VMEM_LIMIT = 56 * 1024 * 1024

_BF16 = jnp.bfloat16
_F32 = jnp.float32


def _dot(a, b):
    return jnp.dot(a.astype(_BF16), b.astype(_BF16), preferred_element_type=_F32)


def _dot_nt(a, b):
    return lax.dot_general(a.astype(_BF16), b.astype(_BF16), (((1,), (1,)), ((), ())),
                           preferred_element_type=_F32)


def _dot_tn(a, b):
    return lax.dot_general(a.astype(_BF16), b.astype(_BF16), (((0,), (0,)), ((), ())),
                           preferred_element_type=_F32)


def _dot_sel(sel, x, *, transpose_x=False):
    dims = (((1,), (1 if transpose_x else 0,)), ((), ()))
    hi = x.astype(_BF16)
    rem = x - hi.astype(_F32)
    mid = rem.astype(_BF16)
    lo = (rem - mid.astype(_F32)).astype(_BF16)
    s = sel.astype(_BF16)
    return sum(lax.dot_general(s, p, dims, preferred_element_type=_F32) for p in (hi, mid, lo))


def _rms_rows(x, g):
    return x * lax.rsqrt(jnp.mean(x * x, axis=-1, keepdims=True) + EPS) * g


def _silu(x):
    return x * jax.nn.sigmoid(x)


def _round_robin(gens, stagger=0):
    live = dict(enumerate(gens))
    rnd = 0
    while live:
        for i in [i for i in live if rnd >= i * stagger]:
            if next(live[i], True) is not None:
                del live[i]
        rnd += 1


def _tap_rows(w_ref, col, width):
    return [jnp.broadcast_to(w_ref[k:k + 1, col:col + width], (SUBLANES, width))
            for k in range(w_ref.shape[0])]


def _short_conv_tile(buf, r0, rc, w_rows):
    taps = len(w_rows)
    n = rc // SUBLANES
    groups = [buf[r0 + SUBLANES * i:r0 + SUBLANES * (i + 1), :] for i in range(n + 1)]
    sub = lax.broadcasted_iota(jnp.int32, groups[0].shape, 0)
    ys = None
    for k in range(taps):
        shift = taps - 1 - k
        if shift:
            rolled = [pltpu.roll(p, shift, axis=0) for p in groups]
            src = [jnp.where(sub < shift, rolled[i], rolled[i + 1]) for i in range(n)]
        else:
            src = groups[1:]
        terms = [p * w_rows[k] for p in src]
        ys = terms if ys is None else [a + b for a, b in zip(ys, terms)]
    return jnp.concatenate(ys, axis=0)


def _resident(shape):
    nd = len(shape)
    return pl.BlockSpec(shape, lambda *_: (0,) * nd, pipeline_mode=pl.Buffered(1))


def _resident_layer(stack, layer, width=None):
    _, r, c = stack.shape
    return pl.BlockSpec((None, r, width or c), lambda *_: (layer, 0, 0), pipeline_mode=pl.Buffered(1))


def _params(n_axes=2):
    sem = ("parallel",) + ("arbitrary",) * (n_axes - 1)
    return pltpu.CompilerParams(dimension_semantics=sem, vmem_limit_bytes=VMEM_LIMIT)


def _ffn_kernel(x_ref, g_ref, wup_ref, wdw_ref, wdown_ref, o_ref, carry_ref, ubuf_ref, act_ref, *,
                tm, d_ff, cw, rc, dg, ms):
    t = pl.program_id(1)

    @pl.when(t == 0)
    def _():
        carry_ref[...] = jnp.zeros_like(carry_ref)

    x = x_ref[...]
    h = _rms_rows(x, g_ref[...]).astype(_BF16)
    acc = x
    taps = wdw_ref.shape[0]
    n_chunks = d_ff // cw
    n_buf = ubuf_ref.shape[0]

    def up(c):
        cols = [half * d_ff + c * cw for half in range(2)]
        bufs = [ubuf_ref.at[(2 * c + half) % n_buf] for half in range(2)]
        for col, buf in zip(cols, bufs):
            buf[0:SUBLANES, :] = carry_ref[:, col:col + cw]
        for m0 in range(0, tm, ms):
            for col, buf in zip(cols, bufs):
                buf[SUBLANES + m0:SUBLANES + m0 + ms, :] = jnp.dot(
                    h[m0:m0 + ms], wup_ref[:, col:col + cw], preferred_element_type=_F32)
            yield
        for col, buf in zip(cols, bufs):
            carry_ref[:, col:col + cw] = buf[tm:tm + SUBLANES, :]

    def epilogue(c):
        w_gate, w_lin = tap_rows(c, 0), tap_rows(c, 1)
        for r0 in range(0, tm, rc):
            act_ref[r0:r0 + rc, c * cw:(c + 1) * cw] = (
                _silu(conv(c, 0, r0, w_gate)) * conv(c, 1, r0, w_lin)).astype(_BF16)
            if (r0 + rc) % ms == 0:
                yield

    def tap_rows(c, half):
        col = half * d_ff + c * cw
        return _tap_rows(wdw_ref, col, cw)

    def conv(c, half, r0, w_rows):
        return _short_conv_tile(ubuf_ref.at[(2 * c + half) % n_buf], r0, rc, w_rows)

    group_ends = set(range(dg - 1, n_chunks, dg)) | {n_chunks - 1}
    group_start = 0
    _round_robin([up(0)])
    for c in range(n_chunks):
        _round_robin(([up(c + 1)] if c + 1 < n_chunks else []) + [epilogue(c)])
        if c in group_ends:
            k0, k1 = group_start * cw, (c + 1) * cw
            acc = acc + jnp.dot(act_ref[:, k0:k1], wdown_ref[k0:k1, :], preferred_element_type=_F32)
            group_start = c + 1
    o_ref[...] = acc


def _conv_ffn(x, g, w_up, w_dw, w_down, layer, *, batch, tm=1024, cw=256, rc=64, dg=4, ms=256):
    m, d = x.shape
    seq = m // batch
    d_ff = w_down.shape[1]
    nt = seq // tm
    row = lambda b, t: (b * nt + t, 0)
    return pl.pallas_call(
        functools.partial(_ffn_kernel, tm=tm, d_ff=d_ff, cw=cw, rc=rc, dg=dg, ms=min(ms, tm)),
        grid=(batch, nt),
        in_specs=[pl.BlockSpec((tm, d), row), _resident((1, d)), _resident_layer(w_up, layer),
                  _resident(w_dw.shape), _resident_layer(w_down, layer)],
        out_specs=pl.BlockSpec((tm, d), row),
        out_shape=jax.ShapeDtypeStruct((m, d), _F32),
        scratch_shapes=[pltpu.VMEM((SUBLANES, 2 * d_ff), _F32),
                        pltpu.VMEM((4, tm + SUBLANES, cw), _F32),
                        pltpu.VMEM((tm, d_ff), _BF16)],
        compiler_params=_params(),
        name="conv_ffn",
    )(x, g.reshape(1, d), w_up, w_dw, w_down)


def _dwconv_tiles(ubuf_ref, w_ref, b_ref, y_ref, *, tm, halo, rc):
    taps, d = w_ref.shape
    base = halo - (taps - 1)
    slabs = -(-(base + taps) // SUBLANES)
    n_out = rc // SUBLANES
    sub = lax.broadcasted_iota(jnp.int32, (SUBLANES, LANES), 0)

    def body(i, carry):
        r0 = pl.multiple_of(i * rc, rc)
        for c0 in range(0, d, LANES):
            cols = slice(c0, c0 + LANES)
            grp = [ubuf_ref[pl.ds(r0 + SUBLANES * j, SUBLANES), cols] for j in range(n_out + slabs - 1)]
            bias = jnp.broadcast_to(b_ref[:, cols], (SUBLANES, LANES))
            acc = [bias] * n_out
            for r in range(SUBLANES):
                ks = [(a, SUBLANES * a + r - base) for a in range(slabs)
                      if 0 <= SUBLANES * a + r - base < taps]
                if not ks:
                    continue
                n_z = n_out if r == 0 else n_out + 1
                z = []
                for j in range(n_z):
                    terms = [grp[j + a] * w_ref[k:k + 1, cols] for a, k in ks]
                    z.append(functools.reduce(lambda p, q: p + q, terms))
                if r == 0:
                    acc = [p + q for p, q in zip(acc, z)]
                else:
                    rolled = [pltpu.roll(zj, SUBLANES - r, axis=0) for zj in z]
                    acc = [p + jnp.where(sub < SUBLANES - r, rolled[j], rolled[j + 1])
                           for j, p in enumerate(acc)]
            y_ref[pl.ds(r0, rc), cols] = jnp.concatenate(acc, axis=0)
        return carry

    lax.fori_loop(0, tm // rc, body, 0)


def _conformer_kernel(x_ref, g_ref, win_ref, bin_ref, wdw_ref, bdw_ref, lng_ref, lnb_ref, wout_ref,
                      o_ref, ubuf_ref, y_ref, *, tm, halo, rc, ms):
    t = pl.program_id(1)
    d = x_ref.shape[1]

    @pl.when(t == 0)
    def _():
        ubuf_ref[0:halo, :] = jnp.zeros((halo, d), _F32)

    def glu(m0):
        rows = slice(m0, m0 + ms)
        h = _rms_rows(x_ref[rows, :], g_ref[...]).astype(_BF16)
        yield
        val = jnp.dot(h, win_ref[:, 0:d], preferred_element_type=_F32) + bin_ref[:, 0:d]
        yield
        gate = jnp.dot(h, win_ref[:, d:2 * d], preferred_element_type=_F32) + bin_ref[:, d:2 * d]
        yield
        ubuf_ref[halo + m0:halo + m0 + ms, :] = val * jax.nn.sigmoid(gate)

    def norm_out(m0):
        rows = slice(m0, m0 + ms)
        y = y_ref[rows, :]
        yc = y - jnp.mean(y, axis=-1, keepdims=True)
        var = jnp.mean(yc * yc, axis=-1, keepdims=True)
        z = _silu(yc * lax.rsqrt(var + EPS) * lng_ref[...] + lnb_ref[...]).astype(_BF16)
        yield
        o_ref[rows, :] = x_ref[rows, :] + jnp.dot(z, wout_ref[...], preferred_element_type=_F32)

    _round_robin((glu(m0) for m0 in range(0, tm, ms)), stagger=1)
    _dwconv_tiles(ubuf_ref, wdw_ref, bdw_ref, y_ref, tm=tm, halo=halo, rc=rc)
    ubuf_ref[0:halo, :] = ubuf_ref[tm:tm + halo, :]
    _round_robin((norm_out(m0) for m0 in range(0, tm, ms)), stagger=1)


def _conformer(x, g, w_in, b_in, w_dw, b_dw, ln_g, ln_b, w_out, layer, *, batch, tm=512, rc=64, ms=256):
    m, d = x.shape
    seq = m // batch
    nt = seq // tm
    taps = w_dw.shape[0]
    halo = -(-(taps - 1) // SUBLANES) * SUBLANES
    row = lambda b, t: (b * nt + t, 0)
    return pl.pallas_call(
        functools.partial(_conformer_kernel, tm=tm, halo=halo, rc=rc, ms=min(ms, tm)),
        grid=(batch, nt),
        in_specs=[pl.BlockSpec((tm, d), row), _resident((1, d)), _resident_layer(w_in, layer),
                  _resident((1, 2 * d)), _resident(w_dw.shape), _resident((1, d)),
                  _resident((1, d)), _resident((1, d)), _resident_layer(w_out, layer)],
        out_specs=pl.BlockSpec((tm, d), row),
        out_shape=jax.ShapeDtypeStruct((m, d), _F32),
        scratch_shapes=[pltpu.VMEM((tm + halo, d), _F32), pltpu.VMEM((tm, d), _F32)],
        compiler_params=_params(),
        name="conformer",
    )(x, g.reshape(1, d), w_in, b_in.reshape(1, 2 * d), w_dw, b_dw.reshape(1, d),
      ln_g.reshape(1, d), ln_b.reshape(1, d), w_out)


def _out_proj_kernel(a_ref, w_ref, res_ref, o_ref):
    o_ref[...] = res_ref[...] + jnp.dot(a_ref[...], w_ref[...], preferred_element_type=_F32)


def _out_proj(a, w, layer, res, *, tm=1024):
    m, k = a.shape
    d = w.shape[2]
    return pl.pallas_call(
        _out_proj_kernel,
        grid=(m // tm,),
        in_specs=[pl.BlockSpec((tm, k), lambda i: (i, 0)), _resident_layer(w, layer),
                  pl.BlockSpec((tm, d), lambda i: (i, 0))],
        out_specs=pl.BlockSpec((tm, d), lambda i: (i, 0)),
        out_shape=jax.ShapeDtypeStruct((m, d), _F32),
        compiler_params=_params(1),
        name="out_proj",
    )(a, w, res)


def _gdn_in_kernel(x_ref, g_ref, w_ref, wab_ref, cw_ref, qkv_ref, z_ref, ab_ref, carry_ref, *ubuf_refs,
                   tm, cwid, n_conv, rc, ms):
    t = pl.program_id(1)

    @pl.when(t == 0)
    def _():
        carry_ref[...] = jnp.zeros_like(carry_ref)

    h = _rms_rows(x_ref[...], g_ref[...]).astype(_BF16)
    n_chunks = n_conv // cwid
    n_buf = len(ubuf_refs)

    def up(c):
        col = c * cwid
        buf = ubuf_refs[c % n_buf]
        buf[0:SUBLANES, :] = carry_ref[:, col:col + cwid]
        for m0 in range(0, tm, ms):
            buf[SUBLANES + m0:SUBLANES + m0 + ms, :] = jnp.dot(
                h[m0:m0 + ms], w_ref[:, col:col + cwid], preferred_element_type=_F32)
            yield
        carry_ref[:, col:col + cwid] = buf[tm:tm + SUBLANES, :]

    def epilogue(c):
        w_rows = _tap_rows(cw_ref, c * cwid, cwid)
        for r0 in range(0, tm, rc):
            qkv_ref[r0:r0 + rc, c * cwid:(c + 1) * cwid] = _silu(
                _short_conv_tile(ubuf_refs[c % n_buf], r0, rc, w_rows))
            if (r0 + rc) % ms == 0:
                yield

    def tail():
        z_ref[...] = jnp.dot(h, w_ref[:, n_conv:], preferred_element_type=_F32)
        ab_ref[...] = jnp.dot(h, wab_ref[...], preferred_element_type=_F32)
        yield

    _round_robin([up(0)])
    for c in range(n_chunks):
        _round_robin([up(c + 1) if c + 1 < n_chunks else tail(), epilogue(c)])


def _gdn_in(x, g, w_in, layer, conv_w, *, batch, tm=512, cwid=256, rc=64, ms=128):
    m, d = x.shape
    seq = m // batch
    nt = seq // tm
    n_conv = conv_w.shape[1]
    n_main = n_conv + d
    n_ab = w_in.shape[2] - n_main
    w_ab = jnp.pad(w_in[layer, :, n_main:], ((0, 0), (0, LANES - n_ab)))
    row = lambda b, t: (b * nt + t, 0)
    return pl.pallas_call(
        functools.partial(_gdn_in_kernel, tm=tm, cwid=cwid, n_conv=n_conv, rc=rc, ms=ms),
        grid=(batch, nt),
        in_specs=[pl.BlockSpec((tm, d), row), _resident((1, d)), _resident_layer(w_in, layer, n_main),
                  _resident(w_ab.shape), _resident(conv_w.shape)],
        out_specs=[pl.BlockSpec((tm, n_conv), row), pl.BlockSpec((tm, d), row),
                   pl.BlockSpec((tm, LANES), row)],
        out_shape=[jax.ShapeDtypeStruct((m, n_conv), _F32), jax.ShapeDtypeStruct((m, d), _F32),
                   jax.ShapeDtypeStruct((m, LANES), _F32)],
        scratch_shapes=[pltpu.VMEM((SUBLANES, n_conv), _F32),
                        *[pltpu.VMEM((tm + SUBLANES, cwid), _F32)] * 2],
        compiler_params=_params(),
        name="gdn_in",
    )(x, g.reshape(1, d), w_in, w_ab, conv_w)


def _gdn_chunk_kernel(qkv_ref, z_ref, ab_ref, alog_ref, dtb_ref, ong_ref, o_ref, state_ref, *,
                      ts, heads, group):
    t = pl.program_id(1)
    C, Dh = GDN_CHUNK, HEAD_DIM
    W = heads * Dh
    shift = C.bit_length() - 1

    @pl.when(t == 0)
    def _():
        state_ref[...] = jnp.zeros_like(state_ref)

    ri = lax.broadcasted_iota(jnp.int32, (ts, ts), 0)
    ci = lax.broadcasted_iota(jnp.int32, (ts, ts), 1)
    same = (ri >> shift) == (ci >> shift)
    dist = jnp.where(same, ri - ci, -1)
    lower = dist >= 0
    strict = dist > 0
    eye = (dist == 0).astype(_F32)
    sel = (lax.broadcasted_iota(jnp.int32, (SUBLANES, LANES), 0)
           == lax.broadcasted_iota(jnp.int32, (SUBLANES, LANES), 1)).astype(_F32)

    ab = ab_ref[...]
    g_raw = -jnp.exp(alog_ref[...]) * jax.nn.softplus(ab + dtb_ref[...])
    beta_all = jax.nn.sigmoid(ab)
    gc = _dot_sel(lower, g_raw)
    gtot = _dot_sel(same, g_raw)
    gc_rows = _dot_sel(sel, gc, transpose_x=True)
    eg_all = jnp.exp(gc)
    egl_all = jnp.exp(gtot)
    ekd_all = jnp.exp(gtot - gc)
    ong = ong_ref[...]

    def head(hd):
        lo = hd * Dh
        q = qkv_ref[:, lo:lo + Dh]
        k = qkv_ref[:, W + lo:W + lo + Dh]
        v = qkv_ref[:, 2 * W + lo:2 * W + lo + Dh]
        q = q * lax.rsqrt(jnp.sum(q * q, axis=-1, keepdims=True) + EPS) * (Dh ** -0.5)
        k = k * lax.rsqrt(jnp.sum(k * k, axis=-1, keepdims=True) + EPS)
        gcol = gc[:, hd:hd + 1]
        grow = gc_rows[hd:hd + 1, :]
        eg = eg_all[:, hd:hd + 1]
        beta = beta_all[:, heads + hd:heads + hd + 1]
        kb = k * beta
        vb = v * beta
        decay = jnp.where(lower, jnp.exp(jnp.where(lower, gcol - grow, 0.0)), 0.0)
        a_mat = jnp.where(strict, _dot_nt(kb, k) * decay, 0.0)
        qk = jnp.where(lower, _dot_nt(q, k) * decay, 0.0)
        yield
        t_mat = eye - a_mat
        pw = a_mat
        for _ in range(shift - 1):
            pw = _dot(pw, pw)
            yield
            t_mat = t_mat + _dot(t_mat, pw)
        yield
        uw = _dot(t_mat, jnp.concatenate([vb, kb * eg], axis=1))
        yield
        u = uw[:, :Dh]
        w = uw[:, Dh:]
        qg = q * eg
        kd = k * ekd_all[:, hd:hd + 1]
        s = state_ref[hd]
        v_new, o_state = [], []
        for n in range(ts // C):
            r = slice(n * C, (n + 1) * C)
            ws = _dot(jnp.concatenate([w[r], qg[r]], axis=0), s)
            yield
            vn = u[r] - ws[:C]
            v_new.append(vn)
            o_state.append(ws[C:])
            s = s * egl_all[n * C:n * C + 1, hd:hd + 1] + _dot_tn(kd[r], vn)
            yield
        state_ref[hd] = s
        o = jnp.concatenate(o_state, axis=0) + _dot(qk, jnp.concatenate(v_new, axis=0))
        o = _rms_rows(o, ong) * _silu(z_ref[:, lo:lo + Dh])
        o_ref[:, lo:lo + Dh] = o.astype(o_ref.dtype)

    for h0 in range(0, heads, group):
        _round_robin(head(hd) for hd in range(h0, min(h0 + group, heads)))


def _gdn_chunk(qkv, z, ab, a_log, dt_bias, o_norm_g, *, batch, heads, ts=256, group=8):
    m, w3 = qkv.shape
    d = z.shape[1]
    seq = m // batch
    nt = seq // ts
    row = lambda b, t: (b * nt + t, 0)
    pad = lambda v: jnp.pad(v.reshape(1, -1), ((0, 0), (0, LANES - v.shape[0])))
    return pl.pallas_call(
        functools.partial(_gdn_chunk_kernel, ts=ts, heads=heads, group=group),
        grid=(batch, nt),
        in_specs=[pl.BlockSpec((ts, w3), row), pl.BlockSpec((ts, d), row),
                  pl.BlockSpec((ts, LANES), row), _resident((1, LANES)), _resident((1, LANES)),
                  _resident((1, HEAD_DIM))],
        out_specs=pl.BlockSpec((ts, d), row),
        out_shape=jax.ShapeDtypeStruct((m, d), _BF16),
        scratch_shapes=[pltpu.VMEM((heads, HEAD_DIM, HEAD_DIM), _F32)],
        compiler_params=_params(),
        name="gdn_chunk",
    )(qkv, z, ab, pad(a_log), pad(dt_bias), o_norm_g.reshape(1, HEAD_DIM))


def _gated_deltanet(x, g, w_in, conv_w, a_log, dt_bias, o_norm_g, w_out, layer, *, batch):
    heads = a_log.shape[0]
    qkv, z, ab = _gdn_in(x, g, w_in, layer, conv_w, batch=batch)
    o = _gdn_chunk(qkv, z, ab, a_log, dt_bias, o_norm_g, batch=batch, heads=heads)
    return _out_proj(o, w_out, layer, x)


def _fox_in_kernel(x_ref, g_ref, w_ref, wf_ref, bf_ref, qg_ref, kg_ref, q_ref, k_ref, v_ref, c_ref,
                   csum_ref, pbuf_ref, *, tm, heads, rc, cb):
    t = pl.program_id(1)
    Dh = HEAD_DIM
    W = heads * Dh

    @pl.when(t == 0)
    def _():
        csum_ref[...] = jnp.zeros_like(csum_ref)

    h = _rms_rows(x_ref[...], g_ref[...]).astype(_BF16)
    scale = Dh ** -0.5 * LOG2_E
    pw = 2 * Dh
    n_qk = 2 * W // pw
    n_buf = pbuf_ref.shape[0]

    def proj(c):
        pbuf_ref[c % n_buf] = jnp.dot(h, w_ref[:, c * pw:(c + 1) * pw], preferred_element_type=_F32)

    def cumsum_gates():
        f = jnp.dot(h, wf_ref[...], preferred_element_type=_F32) + bf_ref[...]
        log_f = jax.nn.log_sigmoid(f) * LOG2_E
        ri = lax.broadcasted_iota(jnp.int32, (cb, cb), 0)
        ci = lax.broadcasted_iota(jnp.int32, (cb, cb), 1)
        tri = ri >= ci
        run = csum_ref[0:1, :]
        for r0 in range(0, tm, cb):
            c = _dot_sel(tri, log_f[r0:r0 + cb, :]) + run
            c_ref[r0:r0 + cb, :] = c
            run = c[cb - 1:cb, :]
        csum_ref[0:1, :] = run

    proj(0)
    for c in range(n_qk):
        if c + 1 < n_qk:
            proj(c + 1)
        else:
            v_ref[...] = jnp.dot(h, w_ref[:, 2 * W:3 * W],
                                 preferred_element_type=_F32).astype(v_ref.dtype)
            cumsum_gates()
        is_q = c < n_qk // 2
        out_ref = q_ref if is_q else k_ref
        gain = qg_ref[...] if is_q else kg_ref[...]
        col0 = (c % (n_qk // 2)) * pw
        for r0 in range(0, tm, rc):
            for lo in range(0, pw, Dh):
                y = _rms_rows(pbuf_ref[c % n_buf, r0:r0 + rc, lo:lo + Dh], gain)
                if is_q:
                    y = y * scale
                out_ref[r0:r0 + rc, col0 + lo:col0 + lo + Dh] = y.astype(out_ref.dtype)


def _fox_in(x, g, w_in, layer, b_f, q_norm_g, k_norm_g, *, batch, heads, tm=512, rc=64, cb=256):
    m, d = x.shape
    seq = m // batch
    nt = seq // tm
    W = heads * HEAD_DIM
    w_f = jnp.pad(w_in[layer, :, 3 * W:], ((0, 0), (0, LANES - heads)))
    b_pad = jnp.pad(b_f.reshape(1, heads), ((0, 0), (0, LANES - heads)))
    row = lambda b, t: (b * nt + t, 0)
    return pl.pallas_call(
        functools.partial(_fox_in_kernel, tm=tm, heads=heads, rc=rc, cb=min(cb, tm)),
        grid=(batch, nt),
        in_specs=[pl.BlockSpec((tm, d), row), _resident((1, d)), _resident_layer(w_in, layer, 3 * W),
                  _resident(w_f.shape), _resident((1, LANES)), _resident((1, HEAD_DIM)),
                  _resident((1, HEAD_DIM))],
        out_specs=[pl.BlockSpec((tm, W), row)] * 3 + [pl.BlockSpec((tm, LANES), row)],
        out_shape=[jax.ShapeDtypeStruct((m, W), _BF16)] * 3 + [jax.ShapeDtypeStruct((m, LANES), _F32)],
        scratch_shapes=[pltpu.VMEM((SUBLANES, LANES), _F32),
                        pltpu.VMEM((3, tm, 2 * HEAD_DIM), _F32)],
        compiler_params=_params(),
        name="fox_in",
    )(x, g.reshape(1, d), w_in, w_f, b_pad, q_norm_g.reshape(1, HEAD_DIM),
      k_norm_g.reshape(1, HEAD_DIM))


def _fox_attn_kernel(q_ref, k_ref, v_ref, ccol_ref, crow_ref, o_ref, m_ref, l_ref, acc_ref, cq_ref, *,
                     tq, heads):
    qi = pl.program_id(1)
    Dh = HEAD_DIM
    ri = lax.broadcasted_iota(jnp.int32, (tq, Dh), 0)
    ci = lax.broadcasted_iota(jnp.int32, (tq, Dh), 1)
    halves = tq // Dh
    causal = [ci + hf * Dh <= ri for hf in range(halves)]
    ones = jnp.ones((tq, Dh), _BF16)

    m_ref[...] = jnp.full(m_ref.shape, -jnp.inf, _F32)
    l_ref[...] = jnp.zeros(l_ref.shape, _F32)
    acc_ref[...] = jnp.zeros(acc_ref.shape, _F32)
    for hd in range(heads):
        cq_ref[hd] = jnp.broadcast_to(ccol_ref[:, hd:hd + 1], (tq, Dh))

    def head_step(hd, j, k0, masked):
        lo = hd * Dh
        s = _dot_nt(q_ref[:, lo:lo + Dh], k_ref[pl.ds(k0, tq), lo:lo + Dh])
        yield
        crow = crow_ref[hd, pl.ds(j, 1), :]
        parts = []
        for hf in range(halves):
            sh = s[:, hf * Dh:(hf + 1) * Dh] - crow[:, hf * Dh:(hf + 1) * Dh]
            parts.append(jnp.where(causal[hf], sh, -jnp.inf) if masked else sh)
        cq = cq_ref[hd]
        m_old = m_ref[hd]
        m_blk = jnp.max(functools.reduce(jnp.maximum, parts), axis=-1, keepdims=True)
        m_new = jnp.maximum(m_old, jnp.broadcast_to(m_blk, (tq, Dh)) + cq)
        yield
        alpha = jnp.exp2(m_old - m_new)
        row = cq - m_new
        p = jnp.concatenate([jnp.exp2(sh + row).astype(_BF16) for sh in parts], axis=1)
        yield
        pv = jnp.dot(p, jnp.concatenate([v_ref[pl.ds(k0, tq), lo:lo + Dh], ones], axis=1),
                     preferred_element_type=_F32)
        yield
        m_ref[hd] = m_new
        l_ref[hd] = alpha * l_ref[hd] + pv[:, Dh:]
        acc_ref[:, lo:lo + Dh] = alpha * acc_ref[:, lo:lo + Dh] + pv[:, :Dh]

    def step(j, masked):
        k0 = pl.multiple_of(j * tq, tq)
        _round_robin([head_step(hd, j, k0, masked) for hd in range(heads)])

    def body(j, carry):
        step(j, False)
        return carry

    lax.fori_loop(0, qi, body, 0)
    step(qi, True)
    for hd in range(heads):
        lo = hd * Dh
        o_ref[:, lo:lo + Dh] = (acc_ref[:, lo:lo + Dh] / l_ref[hd]).astype(o_ref.dtype)


def _fox_attn(q, k, v, c, *, batch, heads, tq=256):
    m, W = q.shape
    seq = m // batch
    nq = seq // tq
    c_rows = c[:, :heads].reshape(batch, seq, heads).transpose(0, 2, 1).reshape(batch, heads, nq, tq)
    row = lambda b, t: (b * nq + t, 0)
    return pl.pallas_call(
        functools.partial(_fox_attn_kernel, tq=tq, heads=heads),
        grid=(batch, nq),
        in_specs=[pl.BlockSpec((tq, W), row), pl.BlockSpec((seq, W), lambda b, t: (b, 0)),
                  pl.BlockSpec((seq, W), lambda b, t: (b, 0)), pl.BlockSpec((tq, LANES), row),
                  pl.BlockSpec((None, heads, nq, tq), lambda b, t: (b, 0, 0, 0))],
        out_specs=pl.BlockSpec((tq, W), row),
        out_shape=jax.ShapeDtypeStruct((m, W), _BF16),
        scratch_shapes=[pltpu.VMEM((heads, tq, HEAD_DIM), _F32), pltpu.VMEM((heads, tq, HEAD_DIM), _F32),
                        pltpu.VMEM((tq, W), _F32), pltpu.VMEM((heads, tq, HEAD_DIM), _F32)],
        compiler_params=_params(),
        name="fox_attn",
    )(q, k, v, c, c_rows)


def _forgetting_attention(x, g, w_in, b_f, q_norm_g, k_norm_g, w_out, layer, *, batch):
    heads = b_f.shape[0]
    q, k, v, c = _fox_in(x, g, w_in, layer, b_f, q_norm_g, k_norm_g, batch=batch, heads=heads)
    o = _fox_attn(q, k, v, c, batch=batch, heads=heads)
    return _out_proj(o, w_out, layer, x)


def kernel(x, mix_norm_g, ffn_norm_g, conv_w_in, conv_b_in, conv_w_dw, conv_b_dw, conv_ln_g, conv_ln_b,
           conv_w_out, gdn_w_in, gdn_conv_w, gdn_a_log, gdn_dt_bias, gdn_o_norm_g, gdn_w_out, fox_w_in,
           fox_b_f, fox_q_norm_g, fox_k_norm_g, fox_w_out, ffn_w_up, ffn_w_dw, ffn_w_down):
    batch, seq, d = x.shape
    depth = mix_norm_g.shape[0]
    xs = x.reshape(batch * seq, d)
    bf = lambda w: w.astype(_BF16)
    conv_w_in, conv_w_out, gdn_w_in, gdn_w_out = bf(conv_w_in), bf(conv_w_out), bf(gdn_w_in), bf(gdn_w_out)
    fox_w_in, fox_w_out, ffn_w_up, ffn_w_down = bf(fox_w_in), bf(fox_w_out), bf(ffn_w_up), bf(ffn_w_down)
    ia = ib = ic = 0
    for layer in range(depth):
        kind = layer % N_MIXERS
        g = mix_norm_g[layer]
        if kind == 0:
            xs = _conformer(xs, g, conv_w_in, conv_b_in[ia], conv_w_dw[ia], conv_b_dw[ia],
                            conv_ln_g[ia], conv_ln_b[ia], conv_w_out, ia, batch=batch)
            ia += 1
        elif kind == 1:
            xs = _gated_deltanet(xs, g, gdn_w_in, gdn_conv_w[ib], gdn_a_log[ib], gdn_dt_bias[ib],
                                 gdn_o_norm_g[ib], gdn_w_out, ib, batch=batch)
            ib += 1
        else:
            xs = _forgetting_attention(xs, g, fox_w_in, fox_b_f[ic], fox_q_norm_g[ic],
                                       fox_k_norm_g[ic], fox_w_out, ic, batch=batch)
            ic += 1
        xs = _conv_ffn(xs, ffn_norm_g[layer], ffn_w_up, ffn_w_dw[layer], ffn_w_down, layer, batch=batch)
    return xs.reshape(batch, seq, d)
```

```python
import functools

import jax
import jax.numpy as jnp
from jax import lax
from jax.experimental import pallas as pl
from jax.experimental.pallas import tpu as pltpu

EPS = 1e-6
LOG2_E = 1.4426950408889634
N_MIXERS = 3
HEAD_DIM = 128
GDN_CHUNK = 64
LANES = 128
SUBLANES = 8
VMEM_LIMIT = 56 * 1024 * 1024

_BF16 = jnp.bfloat16
_F32 = jnp.float32


def _dot(a, b):
    return jnp.dot(a.astype(_BF16), b.astype(_BF16), preferred_element_type=_F32)


def _dot_nt(a, b):
    return lax.dot_general(a.astype(_BF16), b.astype(_BF16), (((1,), (1,)), ((), ())),
                           preferred_element_type=_F32)


def _dot_tn(a, b):
    return lax.dot_general(a.astype(_BF16), b.astype(_BF16), (((0,), (0,)), ((), ())),
                           preferred_element_type=_F32)


def _dot_sel(sel, x, *, transpose_x=False):
    dims = (((1,), (1 if transpose_x else 0,)), ((), ()))
    hi = x.astype(_BF16)
    rem = x - hi.astype(_F32)
    mid = rem.astype(_BF16)
    lo = (rem - mid.astype(_F32)).astype(_BF16)
    s = sel.astype(_BF16)
    return sum(lax.dot_general(s, p, dims, preferred_element_type=_F32) for p in (hi, mid, lo))


def _rms_rows(x, g):
    return x * lax.rsqrt(jnp.mean(x * x, axis=-1, keepdims=True) + EPS) * g


def _silu(x):
    return x * jax.nn.sigmoid(x)


def _round_robin(gens, stagger=0):
    live = dict(enumerate(gens))
    rnd = 0
    while live:
        for i in [i for i in live if rnd >= i * stagger]:
            if next(live[i], True) is not None:
                del live[i]
        rnd += 1


def _tap_rows(w_ref, col, width):
    return [jnp.broadcast_to(w_ref[k:k + 1, col:col + width], (SUBLANES, width))
            for k in range(w_ref.shape[0])]


def _short_conv_tile(buf, r0, rc, w_rows):
    taps = len(w_rows)
    n = rc // SUBLANES
    groups = [buf[r0 + SUBLANES * i:r0 + SUBLANES * (i + 1), :] for i in range(n + 1)]
    sub = lax.broadcasted_iota(jnp.int32, groups[0].shape, 0)
    ys = None
    for k in range(taps):
        shift = taps - 1 - k
        if shift:
            rolled = [pltpu.roll(p, shift, axis=0) for p in groups]
            src = [jnp.where(sub < shift, rolled[i], rolled[i + 1]) for i in range(n)]
        else:
            src = groups[1:]
        terms = [p * w_rows[k] for p in src]
        ys = terms if ys is None else [a + b for a, b in zip(ys, terms)]
    return jnp.concatenate(ys, axis=0)


def _resident(shape):
    nd = len(shape)
    return pl.BlockSpec(shape, lambda *_: (0,) * nd, pipeline_mode=pl.Buffered(1))


def _resident_layer(stack, layer, width=None):
    _, r, c = stack.shape
    return pl.BlockSpec((None, r, width or c), lambda *_: (layer, 0, 0), pipeline_mode=pl.Buffered(1))


def _params(n_axes=2):
    sem = ("parallel",) + ("arbitrary",) * (n_axes - 1)
    return pltpu.CompilerParams(dimension_semantics=sem, vmem_limit_bytes=VMEM_LIMIT)


def _ffn_kernel(x_ref, g_ref, wup_ref, wdw_ref, wdown_ref, o_ref, carry_ref, ubuf_ref, act_ref, *,
                tm, d_ff, cw, rc, dg, ms):
    t = pl.program_id(1)

    @pl.when(t == 0)
    def _():
        carry_ref[...] = jnp.zeros_like(carry_ref)

    x = x_ref[...]
    h = _rms_rows(x, g_ref[...]).astype(_BF16)
    acc = x
    taps = wdw_ref.shape[0]
    n_chunks = d_ff // cw
    n_buf = ubuf_ref.shape[0]

    def up(c):
        cols = [half * d_ff + c * cw for half in range(2)]
        bufs = [ubuf_ref.at[(2 * c + half) % n_buf] for half in range(2)]
        for col, buf in zip(cols, bufs):
            buf[0:SUBLANES, :] = carry_ref[:, col:col + cw]
        for m0 in range(0, tm, ms):
            for col, buf in zip(cols, bufs):
                buf[SUBLANES + m0:SUBLANES + m0 + ms, :] = jnp.dot(
                    h[m0:m0 + ms], wup_ref[:, col:col + cw], preferred_element_type=_F32)
            yield
        for col, buf in zip(cols, bufs):
            carry_ref[:, col:col + cw] = buf[tm:tm + SUBLANES, :]

    def epilogue(c):
        w_gate, w_lin = tap_rows(c, 0), tap_rows(c, 1)
        for r0 in range(0, tm, rc):
            act_ref[r0:r0 + rc, c * cw:(c + 1) * cw] = (
                _silu(conv(c, 0, r0, w_gate)) * conv(c, 1, r0, w_lin)).astype(_BF16)
            if (r0 + rc) % ms == 0:
                yield

    def tap_rows(c, half):
        col = half * d_ff + c * cw
        return _tap_rows(wdw_ref, col, cw)

    def conv(c, half, r0, w_rows):
        return _short_conv_tile(ubuf_ref.at[(2 * c + half) % n_buf], r0, rc, w_rows)

    group_ends = set(range(dg - 1, n_chunks, dg)) | {n_chunks - 1}
    group_start = 0
    _round_robin([up(0)])
    for c in range(n_chunks):
        _round_robin(([up(c + 1)] if c + 1 < n_chunks else []) + [epilogue(c)])
        if c in group_ends:
            k0, k1 = group_start * cw, (c + 1) * cw
            acc = acc + jnp.dot(act_ref[:, k0:k1], wdown_ref[k0:k1, :], preferred_element_type=_F32)
            group_start = c + 1
    o_ref[...] = acc


def _conv_ffn(x, g, w_up, w_dw, w_down, layer, *, batch, tm=1024, cw=256, rc=64, dg=4, ms=256):
    m, d = x.shape
    seq = m // batch
    d_ff = w_down.shape[1]
    nt = seq // tm
    row = lambda b, t: (b * nt + t, 0)
    return pl.pallas_call(
        functools.partial(_ffn_kernel, tm=tm, d_ff=d_ff, cw=cw, rc=rc, dg=dg, ms=min(ms, tm)),
        grid=(batch, nt),
        in_specs=[pl.BlockSpec((tm, d), row), _resident((1, d)), _resident_layer(w_up, layer),
                  _resident(w_dw.shape), _resident_layer(w_down, layer)],
        out_specs=pl.BlockSpec((tm, d), row),
        out_shape=jax.ShapeDtypeStruct((m, d), _F32),
        scratch_shapes=[pltpu.VMEM((SUBLANES, 2 * d_ff), _F32),
                        pltpu.VMEM((4, tm + SUBLANES, cw), _F32),
                        pltpu.VMEM((tm, d_ff), _BF16)],
        compiler_params=_params(),
        name="conv_ffn",
    )(x, g.reshape(1, d), w_up, w_dw, w_down)


def _dwconv_tiles(ubuf_ref, w_ref, b_ref, y_ref, *, tm, halo, rc):
    taps, d = w_ref.shape
    base = halo - (taps - 1)
    slabs = -(-(base + taps) // SUBLANES)
    n_out = rc // SUBLANES
    sub = lax.broadcasted_iota(jnp.int32, (SUBLANES, LANES), 0)

    def body(i, carry):
        r0 = pl.multiple_of(i * rc, rc)
        for c0 in range(0, d, LANES):
            cols = slice(c0, c0 + LANES)
            grp = [ubuf_ref[pl.ds(r0 + SUBLANES * j, SUBLANES), cols] for j in range(n_out + slabs - 1)]
            bias = jnp.broadcast_to(b_ref[:, cols], (SUBLANES, LANES))
            acc = [bias] * n_out
            for r in range(SUBLANES):
                ks = [(a, SUBLANES * a + r - base) for a in range(slabs)
                      if 0 <= SUBLANES * a + r - base < taps]
                if not ks:
                    continue
                n_z = n_out if r == 0 else n_out + 1
                z = []
                for j in range(n_z):
                    terms = [grp[j + a] * w_ref[k:k + 1, cols] for a, k in ks]
                    z.append(functools.reduce(lambda p, q: p + q, terms))
                if r == 0:
                    acc = [p + q for p, q in zip(acc, z)]
                else:
                    rolled = [pltpu.roll(zj, SUBLANES - r, axis=0) for zj in z]
                    acc = [p + jnp.where(sub < SUBLANES - r, rolled[j], rolled[j + 1])
                           for j, p in enumerate(acc)]
            y_ref[pl.ds(r0, rc), cols] = jnp.concatenate(acc, axis=0)
        return carry

    lax.fori_loop(0, tm // rc, body, 0)


def _conformer_kernel(x_ref, g_ref, win_ref, bin_ref, wdw_ref, bdw_ref, lng_ref, lnb_ref, wout_ref,
                      o_ref, ubuf_ref, y_ref, *, tm, halo, rc, ms):
    t = pl.program_id(1)
    d = x_ref.shape[1]

    @pl.when(t == 0)
    def _():
        ubuf_ref[0:halo, :] = jnp.zeros((halo, d), _F32)

    def glu(m0):
        rows = slice(m0, m0 + ms)
        h = _rms_rows(x_ref[rows, :], g_ref[...]).astype(_BF16)
        yield
        val = jnp.dot(h, win_ref[:, 0:d], preferred_element_type=_F32) + bin_ref[:, 0:d]
        yield
        gate = jnp.dot(h, win_ref[:, d:2 * d], preferred_element_type=_F32) + bin_ref[:, d:2 * d]
        yield
        ubuf_ref[halo + m0:halo + m0 + ms, :] = val * jax.nn.sigmoid(gate)

    def norm_out(m0):
        rows = slice(m0, m0 + ms)
        y = y_ref[rows, :]
        yc = y - jnp.mean(y, axis=-1, keepdims=True)
        var = jnp.mean(yc * yc, axis=-1, keepdims=True)
        z = _silu(yc * lax.rsqrt(var + EPS) * lng_ref[...] + lnb_ref[...]).astype(_BF16)
        yield
        o_ref[rows, :] = x_ref[rows, :] + jnp.dot(z, wout_ref[...], preferred_element_type=_F32)

    _round_robin((glu(m0) for m0 in range(0, tm, ms)), stagger=1)
    _dwconv_tiles(ubuf_ref, wdw_ref, bdw_ref, y_ref, tm=tm, halo=halo, rc=rc)
    ubuf_ref[0:halo, :] = ubuf_ref[tm:tm + halo, :]
    _round_robin((norm_out(m0) for m0 in range(0, tm, ms)), stagger=1)


def _conformer(x, g, w_in, b_in, w_dw, b_dw, ln_g, ln_b, w_out, layer, *, batch, tm=1024, rc=64, ms=256):
    m, d = x.shape
    seq = m // batch
    nt = seq // tm
    taps = w_dw.shape[0]
    halo = -(-(taps - 1) // SUBLANES) * SUBLANES
    row = lambda b, t: (b * nt + t, 0)
    return pl.pallas_call(
        functools.partial(_conformer_kernel, tm=tm, halo=halo, rc=rc, ms=min(ms, tm)),
        grid=(batch, nt),
        in_specs=[pl.BlockSpec((tm, d), row), _resident((1, d)), _resident_layer(w_in, layer),
                  _resident((1, 2 * d)), _resident(w_dw.shape), _resident((1, d)),
                  _resident((1, d)), _resident((1, d)), _resident_layer(w_out, layer)],
        out_specs=pl.BlockSpec((tm, d), row),
        out_shape=jax.ShapeDtypeStruct((m, d), _F32),
        scratch_shapes=[pltpu.VMEM((tm + halo, d), _F32), pltpu.VMEM((tm, d), _F32)],
        compiler_params=_params(),
        name="conformer",
    )(x, g.reshape(1, d), w_in, b_in.reshape(1, 2 * d), w_dw, b_dw.reshape(1, d),
      ln_g.reshape(1, d), ln_b.reshape(1, d), w_out)


def _out_proj_kernel(a_ref, w_ref, res_ref, o_ref):
    o_ref[...] = res_ref[...] + jnp.dot(a_ref[...], w_ref[...], preferred_element_type=_F32)


def _out_proj(a, w, layer, res, *, tm=1024):
    m, k = a.shape
    d = w.shape[2]
    return pl.pallas_call(
        _out_proj_kernel,
        grid=(m // tm,),
        in_specs=[pl.BlockSpec((tm, k), lambda i: (i, 0)), _resident_layer(w, layer),
                  pl.BlockSpec((tm, d), lambda i: (i, 0))],
        out_specs=pl.BlockSpec((tm, d), lambda i: (i, 0)),
        out_shape=jax.ShapeDtypeStruct((m, d), _F32),
        compiler_params=_params(1),
        name="out_proj",
    )(a, w, res)


def _gdn_in_kernel(x_ref, g_ref, w_ref, wab_ref, cw_ref, qkv_ref, z_ref, ab_ref, carry_ref, *ubuf_refs,
                   tm, cwid, n_conv, rc, ms):
    t = pl.program_id(1)

    @pl.when(t == 0)
    def _():
        carry_ref[...] = jnp.zeros_like(carry_ref)

    h = _rms_rows(x_ref[...], g_ref[...]).astype(_BF16)
    n_chunks = n_conv // cwid
    n_buf = len(ubuf_refs)

    def up(c):
        col = c * cwid
        buf = ubuf_refs[c % n_buf]
        buf[0:SUBLANES, :] = carry_ref[:, col:col + cwid]
        for m0 in range(0, tm, ms):
            buf[SUBLANES + m0:SUBLANES + m0 + ms, :] = jnp.dot(
                h[m0:m0 + ms], w_ref[:, col:col + cwid], preferred_element_type=_F32)
            yield
        carry_ref[:, col:col + cwid] = buf[tm:tm + SUBLANES, :]

    def epilogue(c):
        w_rows = _tap_rows(cw_ref, c * cwid, cwid)
        for r0 in range(0, tm, rc):
            qkv_ref[r0:r0 + rc, c * cwid:(c + 1) * cwid] = _silu(
                _short_conv_tile(ubuf_refs[c % n_buf], r0, rc, w_rows))
            if (r0 + rc) % ms == 0:
                yield

    def tail():
        z_ref[...] = jnp.dot(h, w_ref[:, n_conv:], preferred_element_type=_F32)
        ab_ref[...] = jnp.dot(h, wab_ref[...], preferred_element_type=_F32)
        yield

    _round_robin([up(0)])
    for c in range(n_chunks):
        _round_robin([up(c + 1) if c + 1 < n_chunks else tail(), epilogue(c)])


def _gdn_in(x, g, w_in, layer, conv_w, *, batch, tm=512, cwid=256, rc=64, ms=512):
    m, d = x.shape
    seq = m // batch
    nt = seq // tm
    n_conv = conv_w.shape[1]
    n_main = n_conv + d
    n_ab = w_in.shape[2] - n_main
    w_ab = jnp.pad(w_in[layer, :, n_main:], ((0, 0), (0, LANES - n_ab)))
    row = lambda b, t: (b * nt + t, 0)
    return pl.pallas_call(
        functools.partial(_gdn_in_kernel, tm=tm, cwid=cwid, n_conv=n_conv, rc=rc, ms=ms),
        grid=(batch, nt),
        in_specs=[pl.BlockSpec((tm, d), row), _resident((1, d)), _resident_layer(w_in, layer, n_main),
                  _resident(w_ab.shape), _resident(conv_w.shape)],
        out_specs=[pl.BlockSpec((tm, n_conv), row), pl.BlockSpec((tm, d), row),
                   pl.BlockSpec((tm, LANES), row)],
        out_shape=[jax.ShapeDtypeStruct((m, n_conv), _F32), jax.ShapeDtypeStruct((m, d), _F32),
                   jax.ShapeDtypeStruct((m, LANES), _F32)],
        scratch_shapes=[pltpu.VMEM((SUBLANES, n_conv), _F32),
                        *[pltpu.VMEM((tm + SUBLANES, cwid), _F32)] * 2],
        compiler_params=_params(),
        name="gdn_in",
    )(x, g.reshape(1, d), w_in, w_ab, conv_w)


def _gdn_chunk_kernel(qkv_ref, z_ref, ab_ref, alog_ref, dtb_ref, ong_ref, o_ref, state_ref, *,
                      ts, heads, group):
    t = pl.program_id(1)
    C, Dh = GDN_CHUNK, HEAD_DIM
    W = heads * Dh
    shift = C.bit_length() - 1

    @pl.when(t == 0)
    def _():
        state_ref[...] = jnp.zeros_like(state_ref)

    ri = lax.broadcasted_iota(jnp.int32, (ts, ts), 0)
    ci = lax.broadcasted_iota(jnp.int32, (ts, ts), 1)
    same = (ri >> shift) == (ci >> shift)
    dist = jnp.where(same, ri - ci, -1)
    lower = dist >= 0
    strict = dist > 0
    eye = (dist == 0).astype(_F32)
    sel = (lax.broadcasted_iota(jnp.int32, (SUBLANES, LANES), 0)
           == lax.broadcasted_iota(jnp.int32, (SUBLANES, LANES), 1)).astype(_F32)

    ab = ab_ref[...]
    g_raw = -jnp.exp(alog_ref[...]) * jax.nn.softplus(ab + dtb_ref[...])
    beta_all = jax.nn.sigmoid(ab)
    gc = _dot_sel(lower, g_raw)
    gtot = _dot_sel(same, g_raw)
    gc_rows = _dot_sel(sel, gc, transpose_x=True)
    eg_all = jnp.exp(gc)
    egl_all = jnp.exp(gtot)
    ekd_all = jnp.exp(gtot - gc)
    ong = ong_ref[...]

    def head(hd):
        lo = hd * Dh
        q = qkv_ref[:, lo:lo + Dh]
        k = qkv_ref[:, W + lo:W + lo + Dh]
        v = qkv_ref[:, 2 * W + lo:2 * W + lo + Dh]
        q = q * lax.rsqrt(jnp.sum(q * q, axis=-1, keepdims=True) + EPS) * (Dh ** -0.5)
        k = k * lax.rsqrt(jnp.sum(k * k, axis=-1, keepdims=True) + EPS)
        gcol = gc[:, hd:hd + 1]
        grow = gc_rows[hd:hd + 1, :]
        eg = eg_all[:, hd:hd + 1]
        beta = beta_all[:, heads + hd:heads + hd + 1]
        kb = k * beta
        vb = v * beta
        decay = jnp.where(lower, jnp.exp(jnp.where(lower, gcol - grow, 0.0)), 0.0)
        a_mat = jnp.where(strict, _dot_nt(kb, k) * decay, 0.0)
        qk = jnp.where(lower, _dot_nt(q, k) * decay, 0.0)
        yield
        t_mat = eye - a_mat
        pw = a_mat
        for _ in range(shift - 1):
            pw = _dot(pw, pw)
            yield
            t_mat = t_mat + _dot(t_mat, pw)
        yield
        uw = _dot(t_mat, jnp.concatenate([vb, kb * eg], axis=1))
        yield
        u = uw[:, :Dh]
        w = uw[:, Dh:]
        qg = q * eg
        kd = k * ekd_all[:, hd:hd + 1]
        s = state_ref[hd]
        v_new, o_state = [], []
        for n in range(ts // C):
            r = slice(n * C, (n + 1) * C)
            ws = _dot(jnp.concatenate([w[r], qg[r]], axis=0), s)
            yield
            vn = u[r] - ws[:C]
            v_new.append(vn)
            o_state.append(ws[C:])
            s = s * egl_all[n * C:n * C + 1, hd:hd + 1] + _dot_tn(kd[r], vn)
            yield
        state_ref[hd] = s
        o = jnp.concatenate(o_state, axis=0) + _dot(qk, jnp.concatenate(v_new, axis=0))
        o = _rms_rows(o, ong) * _silu(z_ref[:, lo:lo + Dh])
        o_ref[:, lo:lo + Dh] = o.astype(o_ref.dtype)

    for h0 in range(0, heads, group):
        _round_robin(head(hd) for hd in range(h0, min(h0 + group, heads)))


def _gdn_chunk(qkv, z, ab, a_log, dt_bias, o_norm_g, *, batch, heads, ts=256, group=8):
    m, w3 = qkv.shape
    d = z.shape[1]
    seq = m // batch
    nt = seq // ts
    row = lambda b, t: (b * nt + t, 0)
    pad = lambda v: jnp.pad(v.reshape(1, -1), ((0, 0), (0, LANES - v.shape[0])))
    return pl.pallas_call(
        functools.partial(_gdn_chunk_kernel, ts=ts, heads=heads, group=group),
        grid=(batch, nt),
        in_specs=[pl.BlockSpec((ts, w3), row), pl.BlockSpec((ts, d), row),
                  pl.BlockSpec((ts, LANES), row), _resident((1, LANES)), _resident((1, LANES)),
                  _resident((1, HEAD_DIM))],
        out_specs=pl.BlockSpec((ts, d), row),
        out_shape=jax.ShapeDtypeStruct((m, d), _BF16),
        scratch_shapes=[pltpu.VMEM((heads, HEAD_DIM, HEAD_DIM), _F32)],
        compiler_params=_params(),
        name="gdn_chunk",
    )(qkv, z, ab, pad(a_log), pad(dt_bias), o_norm_g.reshape(1, HEAD_DIM))


def _gated_deltanet(x, g, w_in, conv_w, a_log, dt_bias, o_norm_g, w_out, layer, *, batch):
    heads = a_log.shape[0]
    qkv, z, ab = _gdn_in(x, g, w_in, layer, conv_w, batch=batch)
    o = _gdn_chunk(qkv, z, ab, a_log, dt_bias, o_norm_g, batch=batch, heads=heads)
    return _out_proj(o, w_out, layer, x)


def _fox_in_kernel(x_ref, g_ref, w_ref, wf_ref, bf_ref, qg_ref, kg_ref, q_ref, k_ref, v_ref, c_ref,
                   csum_ref, pbuf_ref, *, tm, heads, rc, cb):
    t = pl.program_id(1)
    Dh = HEAD_DIM
    W = heads * Dh

    @pl.when(t == 0)
    def _():
        csum_ref[...] = jnp.zeros_like(csum_ref)

    h = _rms_rows(x_ref[...], g_ref[...]).astype(_BF16)
    scale = Dh ** -0.5 * LOG2_E
    pw = 2 * Dh
    n_qk = 2 * W // pw
    n_buf = pbuf_ref.shape[0]

    def proj(c):
        pbuf_ref[c % n_buf] = jnp.dot(h, w_ref[:, c * pw:(c + 1) * pw], preferred_element_type=_F32)

    def cumsum_gates():
        f = jnp.dot(h, wf_ref[...], preferred_element_type=_F32) + bf_ref[...]
        log_f = jax.nn.log_sigmoid(f) * LOG2_E
        ri = lax.broadcasted_iota(jnp.int32, (cb, cb), 0)
        ci = lax.broadcasted_iota(jnp.int32, (cb, cb), 1)
        tri = ri >= ci
        run = csum_ref[0:1, :]
        for r0 in range(0, tm, cb):
            c = _dot_sel(tri, log_f[r0:r0 + cb, :]) + run
            c_ref[r0:r0 + cb, :] = c
            run = c[cb - 1:cb, :]
        csum_ref[0:1, :] = run

    proj(0)
    for c in range(n_qk):
        if c + 1 < n_qk:
            proj(c + 1)
        else:
            v_ref[...] = jnp.dot(h, w_ref[:, 2 * W:3 * W],
                                 preferred_element_type=_F32).astype(v_ref.dtype)
            cumsum_gates()
        is_q = c < n_qk // 2
        out_ref = q_ref if is_q else k_ref
        gain = qg_ref[...] if is_q else kg_ref[...]
        col0 = (c % (n_qk // 2)) * pw
        for r0 in range(0, tm, rc):
            for lo in range(0, pw, Dh):
                y = _rms_rows(pbuf_ref[c % n_buf, r0:r0 + rc, lo:lo + Dh], gain)
                if is_q:
                    y = y * scale
                out_ref[r0:r0 + rc, col0 + lo:col0 + lo + Dh] = y.astype(out_ref.dtype)


def _fox_in(x, g, w_in, layer, b_f, q_norm_g, k_norm_g, *, batch, heads, tm=512, rc=64, cb=256):
    m, d = x.shape
    seq = m // batch
    nt = seq // tm
    W = heads * HEAD_DIM
    w_f = jnp.pad(w_in[layer, :, 3 * W:], ((0, 0), (0, LANES - heads)))
    b_pad = jnp.pad(b_f.reshape(1, heads), ((0, 0), (0, LANES - heads)))
    row = lambda b, t: (b * nt + t, 0)
    return pl.pallas_call(
        functools.partial(_fox_in_kernel, tm=tm, heads=heads, rc=rc, cb=min(cb, tm)),
        grid=(batch, nt),
        in_specs=[pl.BlockSpec((tm, d), row), _resident((1, d)), _resident_layer(w_in, layer, 3 * W),
                  _resident(w_f.shape), _resident((1, LANES)), _resident((1, HEAD_DIM)),
                  _resident((1, HEAD_DIM))],
        out_specs=[pl.BlockSpec((tm, W), row)] * 3 + [pl.BlockSpec((tm, LANES), row)],
        out_shape=[jax.ShapeDtypeStruct((m, W), _BF16)] * 3 + [jax.ShapeDtypeStruct((m, LANES), _F32)],
        scratch_shapes=[pltpu.VMEM((SUBLANES, LANES), _F32),
                        pltpu.VMEM((3, tm, 2 * HEAD_DIM), _F32)],
        compiler_params=_params(),
        name="fox_in",
    )(x, g.reshape(1, d), w_in, w_f, b_pad, q_norm_g.reshape(1, HEAD_DIM),
      k_norm_g.reshape(1, HEAD_DIM))


def _fox_attn_kernel(q_ref, k_ref, v_ref, ccol_ref, crow_ref, o_ref, m_ref, l_ref, acc_ref, cq_ref, *,
                     tq, heads):
    qi = pl.program_id(1)
    Dh = HEAD_DIM
    ri = lax.broadcasted_iota(jnp.int32, (tq, Dh), 0)
    ci = lax.broadcasted_iota(jnp.int32, (tq, Dh), 1)
    halves = tq // Dh
    causal = [ci + hf * Dh <= ri for hf in range(halves)]
    ones = jnp.ones((tq, Dh), _BF16)

    m_ref[...] = jnp.full(m_ref.shape, -jnp.inf, _F32)
    l_ref[...] = jnp.zeros(l_ref.shape, _F32)
    acc_ref[...] = jnp.zeros(acc_ref.shape, _F32)
    for hd in range(heads):
        cq_ref[hd] = jnp.broadcast_to(ccol_ref[:, hd:hd + 1], (tq, Dh))

    def head_step(hd, j, k0, masked):
        lo = hd * Dh
        s = _dot_nt(q_ref[:, lo:lo + Dh], k_ref[pl.ds(k0, tq), lo:lo + Dh])
        yield
        crow = crow_ref[hd, pl.ds(j, 1), :]
        parts = []
        for hf in range(halves):
            sh = s[:, hf * Dh:(hf + 1) * Dh] - crow[:, hf * Dh:(hf + 1) * Dh]
            parts.append(jnp.where(causal[hf], sh, -jnp.inf) if masked else sh)
        cq = cq_ref[hd]
        m_old = m_ref[hd]
        m_blk = jnp.max(functools.reduce(jnp.maximum, parts), axis=-1, keepdims=True)
        m_new = jnp.maximum(m_old, jnp.broadcast_to(m_blk, (tq, Dh)) + cq)
        yield
        alpha = jnp.exp2(m_old - m_new)
        row = cq - m_new
        p = jnp.concatenate([jnp.exp2(sh + row).astype(_BF16) for sh in parts], axis=1)
        yield
        pv = jnp.dot(p, jnp.concatenate([v_ref[pl.ds(k0, tq), lo:lo + Dh], ones], axis=1),
                     preferred_element_type=_F32)
        yield
        m_ref[hd] = m_new
        l_ref[hd] = alpha * l_ref[hd] + pv[:, Dh:]
        acc_ref[:, lo:lo + Dh] = alpha * acc_ref[:, lo:lo + Dh] + pv[:, :Dh]

    def step(j, masked):
        k0 = pl.multiple_of(j * tq, tq)
        _round_robin([head_step(hd, j, k0, masked) for hd in range(heads)])

    def body(j, carry):
        step(j, False)
        return carry

    lax.fori_loop(0, qi, body, 0)
    step(qi, True)
    for hd in range(heads):
        lo = hd * Dh
        o_ref[:, lo:lo + Dh] = (acc_ref[:, lo:lo + Dh] / l_ref[hd]).astype(o_ref.dtype)


def _fox_attn(q, k, v, c, *, batch, heads, tq=256):
    m, W = q.shape
    seq = m // batch
    nq = seq // tq
    c_rows = c[:, :heads].reshape(batch, seq, heads).transpose(0, 2, 1).reshape(batch, heads, nq, tq)
    row = lambda b, t: (b * nq + t, 0)
    return pl.pallas_call(
        functools.partial(_fox_attn_kernel, tq=tq, heads=heads),
        grid=(batch, nq),
        in_specs=[pl.BlockSpec((tq, W), row), pl.BlockSpec((seq, W), lambda b, t: (b, 0)),
                  pl.BlockSpec((seq, W), lambda b, t: (b, 0)), pl.BlockSpec((tq, LANES), row),
                  pl.BlockSpec((None, heads, nq, tq), lambda b, t: (b, 0, 0, 0))],
        out_specs=pl.BlockSpec((tq, W), row),
        out_shape=jax.ShapeDtypeStruct((m, W), _BF16),
        scratch_shapes=[pltpu.VMEM((heads, tq, HEAD_DIM), _F32), pltpu.VMEM((heads, tq, HEAD_DIM), _F32),
                        pltpu.VMEM((tq, W), _F32), pltpu.VMEM((heads, tq, HEAD_DIM), _F32)],
        compiler_params=_params(),
        name="fox_attn",
    )(q, k, v, c, c_rows)


def _forgetting_attention(x, g, w_in, b_f, q_norm_g, k_norm_g, w_out, layer, *, batch):
    heads = b_f.shape[0]
    q, k, v, c = _fox_in(x, g, w_in, layer, b_f, q_norm_g, k_norm_g, batch=batch, heads=heads)
    o = _fox_attn(q, k, v, c, batch=batch, heads=heads)
    return _out_proj(o, w_out, layer, x)


def kernel(x, mix_norm_g, ffn_norm_g, conv_w_in, conv_b_in, conv_w_dw, conv_b_dw, conv_ln_g, conv_ln_b,
           conv_w_out, gdn_w_in, gdn_conv_w, gdn_a_log, gdn_dt_bias, gdn_o_norm_g, gdn_w_out, fox_w_in,
           fox_b_f, fox_q_norm_g, fox_k_norm_g, fox_w_out, ffn_w_up, ffn_w_dw, ffn_w_down):
    batch, seq, d = x.shape
    depth = mix_norm_g.shape[0]
    xs = x.reshape(batch * seq, d)
    bf = lambda w: w.astype(_BF16)
    conv_w_in, conv_w_out, gdn_w_in, gdn_w_out = bf(conv_w_in), bf(conv_w_out), bf(gdn_w_in), bf(gdn_w_out)
    fox_w_in, fox_w_out, ffn_w_up, ffn_w_down = bf(fox_w_in), bf(fox_w_out), bf(ffn_w_up), bf(ffn_w_down)
    ia = ib = ic = 0
    for layer in range(depth):
        kind = layer % N_MIXERS
        g = mix_norm_g[layer]
        if kind == 0:
            xs = _conformer(xs, g, conv_w_in, conv_b_in[ia], conv_w_dw[ia], conv_b_dw[ia],
                            conv_ln_g[ia], conv_ln_b[ia], conv_w_out, ia, batch=batch)
            ia += 1
        elif kind == 1:
            xs = _gated_deltanet(xs, g, gdn_w_in, gdn_conv_w[ib], gdn_a_log[ib], gdn_dt_bias[ib],
                                 gdn_o_norm_g[ib], gdn_w_out, ib, batch=batch)
            ib += 1
        else:
            xs = _forgetting_attention(xs, g, fox_w_in, fox_b_f[ic], fox_q_norm_g[ic],
                                       fox_k_norm_g[ic], fox_w_out, ic, batch=batch)
            ic += 1
        xs = _conv_ffn(xs, ffn_norm_g[layer], ffn_w_up, ffn_w_dw[layer], ffn_w_down, layer, batch=batch)
    return xs.reshape(batch, seq, d)
```
